```python
import jax, jax.numpy as jnp
from jax import lax
import numpy as np

D_MODEL = 2048
BATCH = 1
SEQ = 16384
DEPTH = 2

D_MIX = D_MODEL
C_CONV = D_MIX // 4
C_SSM = D_MIX // 4
D_ATT = D_MIX - C_CONV - C_SSM
HEAD_DIM = 128
N_HEADS = D_ATT // HEAD_DIM
CONV_WIDTH = 31
SSM_GROUP_CH = 16
SSM_GROUPS = C_SSM // SSM_GROUP_CH
SSM_STATE = 64
N_IN = 2 * C_CONV + C_SSM + 3 * D_ATT + N_HEADS
N_EXPERTS = 32
TOP_K = 4
D_FF = D_MODEL
SWIGLU_LIMIT = 7.0
SWIGLU_ALPHA = 1.702
Q_BLOCK = 128
EXPERT_BLOCK = 128
EPS = 1e-6
F32 = jnp.float32

kernel_name = "hybrid_conv_s5_fox_moe"


def _rmsnorm(x, g):
    xf = x.astype(F32)
    y = xf * lax.rsqrt(jnp.mean(xf * xf, axis=-1, keepdims=True) + EPS)
    return (y * g.astype(F32)).astype(x.dtype)


def _layernorm(x, g, b):
    xf = x.astype(F32)
    mu = jnp.mean(xf, axis=-1, keepdims=True)
    xc = xf - mu
    y = xc * lax.rsqrt(jnp.mean(xc * xc, axis=-1, keepdims=True) + EPS)
    return (y * g.astype(F32) + b.astype(F32)).astype(x.dtype)


def _conv_module(c_in, conv_w, conv_b, ln_g, ln_b):
    a, gate = jnp.split(c_in, 2, axis=-1)
    u = a * jax.nn.sigmoid(gate)
    y = lax.conv_general_dilated(
        u, conv_w[:, None, :].astype(u.dtype),
        window_strides=(1,), padding=[(CONV_WIDTH - 1, 0)],
        dimension_numbers=('NWC', 'WIO', 'NWC'), feature_group_count=C_CONV)
    y = y + conv_b.astype(y.dtype)
    y = _layernorm(y, ln_g, ln_b)
    return jax.nn.silu(y)


def _cmul(ar, ai, br, bi):
    return ar * br - ai * bi, ar * bi + ai * br


def _ssm_combine(e1, e2):
    a1r, a1i, b1r, b1i = e1
    a2r, a2i, b2r, b2i = e2
    ar, ai = _cmul(a2r, a2i, a1r, a1i)
    br, bi = _cmul(a2r, a2i, b1r, b1i)
    return ar, ai, br + b2r, bi + b2i


def _s5_module(u, A_re, A_im, log_dt, B_re, B_im, C_re, C_im, D, glu_w, glu_b):
    bsz, L, _ = u.shape
    uf = u.astype(F32).reshape(bsz, L, SSM_GROUPS, SSM_GROUP_CH)
    A_re = A_re.astype(F32)
    A_im = A_im.astype(F32)
    dt = jnp.exp(log_dt.astype(F32))[:, None]
    mag = jnp.exp(dt * A_re)
    ang = dt * A_im
    ab_re, ab_im = mag * jnp.cos(ang), mag * jnp.sin(ang)
    inv = 1.0 / (A_re * A_re + A_im * A_im)
    fr, fi = _cmul(ab_re - 1.0, ab_im, A_re * inv, -A_im * inv)
    bb_re, bb_im = _cmul(fr[..., None], fi[..., None], B_re.astype(F32), B_im.astype(F32))
    bu_re = jnp.einsum('blgh,gph->blgp', uf, bb_re)
    bu_im = jnp.einsum('blgh,gph->blgp', uf, bb_im)
    a_re = jnp.broadcast_to(ab_re, bu_re.shape)
    a_im = jnp.broadcast_to(ab_im, bu_im.shape)
    _, _, x_re, x_im = lax.associative_scan(_ssm_combine, (a_re, a_im, bu_re, bu_im), axis=1)
    y = (jnp.einsum('blgp,ghp->blgh', x_re, C_re.astype(F32))
         - jnp.einsum('blgp,ghp->blgh', x_im, C_im.astype(F32)))
    y = y + D.astype(F32).reshape(SSM_GROUPS, SSM_GROUP_CH) * uf
    y = jax.nn.gelu(y.reshape(bsz, L, C_SSM))
    y = y * jax.nn.sigmoid(y @ glu_w.astype(F32) + glu_b.astype(F32))
    return y.astype(u.dtype)


def _fox_attention(q, k, v, log_f):
    bsz, L, H, Dh = q.shape
    nb = L // Q_BLOCK
    scale = Dh ** -0.5
    c = jnp.cumsum(log_f.astype(F32), axis=1).transpose(0, 2, 1)
    qb = q.reshape(bsz, nb, Q_BLOCK, H, Dh).transpose(1, 0, 2, 3, 4)
    cb = c.reshape(bsz, H, nb, Q_BLOCK).transpose(2, 0, 1, 3)
    k_pos = jnp.arange(L)

    def block(args):
        i, qi, ci = args
        s = jnp.einsum('bqhd,bkhd->bhqk', qi, k).astype(F32) * scale
        s = s + ci[..., :, None] - c[..., None, :]
        q_pos = i * Q_BLOCK + jnp.arange(Q_BLOCK)
        s = jnp.where(k_pos[None, :] <= q_pos[:, None], s, -jnp.inf)
        p = jax.nn.softmax(s, axis=-1)
        return jnp.einsum('bhqk,bkhd->bqhd', p.astype(v.dtype), v)

    o = lax.map(block, (jnp.arange(nb), qb, cb))
    return o.transpose(1, 0, 2, 3, 4).reshape(bsz, L, H * Dh)


def _hybrid_mixer(h, w_in, b_forget, conv_w, conv_b, conv_ln_g, conv_ln_b,
                  A_re, A_im, log_dt, B_re, B_im, C_re, C_im, D, glu_w, glu_b,
                  q_gain, k_gain, branch_gain, w_out):
    bsz, L, _ = h.shape
    z = h @ w_in
    i1 = 2 * C_CONV
    i2 = i1 + C_SSM
    i3 = i2 + D_ATT
    i4 = i3 + D_ATT
    i5 = i4 + D_ATT
    c_in, s_in, q, k, v, f_logit = jnp.split(z, [i1, i2, i3, i4, i5], axis=-1)
    y_conv = _conv_module(c_in, conv_w, conv_b, conv_ln_g, conv_ln_b)
    y_ssm = _s5_module(s_in, A_re, A_im, log_dt, B_re, B_im, C_re, C_im, D, glu_w, glu_b)
    q = _rmsnorm(q.reshape(bsz, L, N_HEADS, HEAD_DIM), q_gain)
    k = _rmsnorm(k.reshape(bsz, L, N_HEADS, HEAD_DIM), k_gain)
    v = v.reshape(bsz, L, N_HEADS, HEAD_DIM)
    log_f = jax.nn.log_sigmoid((f_logit + b_forget).astype(F32))
    y_att = _fox_attention(q, k, v, log_f)
    g_conv, g_ssm, g_att = jnp.split(branch_gain, [C_CONV, C_CONV + C_SSM])
    y = jnp.concatenate([_rmsnorm(y_conv, g_conv), _rmsnorm(y_ssm, g_ssm),
                         _rmsnorm(y_att, g_att)], axis=-1)
    return y @ w_out


def _moe(h, router_w, router_b, w_gu, b_gu, w_dn, b_dn):
    bsz, L, Dm = h.shape
    T = bsz * L
    xt = h.reshape(T, Dm)
    logits = (xt @ router_w + router_b).astype(F32)
    top_val, top_idx = lax.top_k(logits, TOP_K)
    gates = jax.nn.softmax(top_val, axis=-1)
    flat_e = top_idx.reshape(-1)
    flat_tok = jnp.repeat(jnp.arange(T, dtype=jnp.int32), TOP_K)
    flat_g = gates.reshape(-1)
    order = jnp.argsort(flat_e, stable=True)
    s_e, s_tok, s_g = flat_e[order], flat_tok[order], flat_g[order]
    counts = jnp.bincount(flat_e, length=N_EXPERTS)
    padded = (counts + EXPERT_BLOCK - 1) // EXPERT_BLOCK * EXPERT_BLOCK
    start = jnp.cumsum(counts) - counts
    pend = jnp.cumsum(padded)
    pstart = pend - padded
    dest = pstart[s_e] + (jnp.arange(T * TOP_K) - start[s_e])
    n_blocks = -(-(T * TOP_K) // EXPERT_BLOCK) + N_EXPERTS
    R = n_blocks * EXPERT_BLOCK
    buf_tok = jnp.zeros((R,), jnp.int32).at[dest].set(s_tok)
    buf_g = jnp.zeros((R,), F32).at[dest].set(s_g)
    block_e = jnp.minimum(
        jnp.searchsorted(pend, jnp.arange(n_blocks) * EXPERT_BLOCK, side='right'),
        N_EXPERTS - 1)

    def expert_block(args):
        e, tok = args
        xb = xt[tok]
        hgu = xb @ w_gu[e] + b_gu[e]
        g = jnp.minimum(hgu[:, 0::2], SWIGLU_LIMIT)
        u = jnp.clip(hgu[:, 1::2], -SWIGLU_LIMIT, SWIGLU_LIMIT)
        a = g * jax.nn.sigmoid(SWIGLU_ALPHA * g) * (u + 1.0)
        return a @ w_dn[e] + b_dn[e]

    ys = lax.map(expert_block, (block_e, buf_tok.reshape(n_blocks, EXPERT_BLOCK)))
    ys = ys.reshape(R, Dm) * buf_g[:, None].astype(ys.dtype)
    out = jax.ops.segment_sum(ys, buf_tok, num_segments=T)
    return out.reshape(bsz, L, Dm)


def setup_inputs(seed: int = 0) -> dict:
    key = jax.random.key(seed)
    ks = jax.random.split(key, 32)

    def nrm(k, shape, scale):
        return jax.random.normal(k, shape, F32) * scale

    n_idx = jnp.arange(SSM_STATE, dtype=F32)
    return {
        'x': nrm(ks[0], (BATCH, SEQ, D_MODEL), 1.0),
        'norm_mix': 1.0 + nrm(ks[1], (DEPTH, D_MODEL), 0.05),
        'w_in': nrm(ks[2], (DEPTH, D_MODEL, N_IN), D_MODEL ** -0.5),
        'b_forget': 3.0 + nrm(ks[3], (DEPTH, N_HEADS), 0.5),
        'conv_w': nrm(ks[4], (DEPTH, CONV_WIDTH, C_CONV), CONV_WIDTH ** -0.5),
        'conv_b': nrm(ks[5], (DEPTH, C_CONV), 0.01),
        'conv_ln_g': 1.0 + nrm(ks[6], (DEPTH, C_CONV), 0.05),
        'conv_ln_b': nrm(ks[7], (DEPTH, C_CONV), 0.01),
        'ssm_A_re': -0.5 + nrm(ks[8], (DEPTH, SSM_GROUPS, SSM_STATE), 0.01),
        'ssm_A_im': jnp.pi * n_idx + nrm(ks[9], (DEPTH, SSM_GROUPS, SSM_STATE), 0.01),
        'ssm_log_dt': jax.random.uniform(ks[10], (DEPTH, SSM_GROUPS), F32,
                                         minval=float(np.log(1e-3)), maxval=float(np.log(1e-1))),
        'ssm_B_re': nrm(ks[11], (DEPTH, SSM_GROUPS, SSM_STATE, SSM_GROUP_CH), (2 * SSM_GROUP_CH) ** -0.5),
        'ssm_B_im': nrm(ks[12], (DEPTH, SSM_GROUPS, SSM_STATE, SSM_GROUP_CH), (2 * SSM_GROUP_CH) ** -0.5),
        'ssm_C_re': nrm(ks[13], (DEPTH, SSM_GROUPS, SSM_GROUP_CH, SSM_STATE), (2 * SSM_STATE) ** -0.5),
        'ssm_C_im': nrm(ks[14], (DEPTH, SSM_GROUPS, SSM_GROUP_CH, SSM_STATE), (2 * SSM_STATE) ** -0.5),
        'ssm_D': nrm(ks[15], (DEPTH, C_SSM), 1.0),
        'ssm_glu_w': nrm(ks[16], (DEPTH, C_SSM, C_SSM), C_SSM ** -0.5),
        'ssm_glu_b': nrm(ks[17], (DEPTH, C_SSM), 0.01),
        'q_norm': 1.0 + nrm(ks[18], (DEPTH, HEAD_DIM), 0.05),
        'k_norm': 1.0 + nrm(ks[19], (DEPTH, HEAD_DIM), 0.05),
        'branch_norm': 1.0 + nrm(ks[20], (DEPTH, D_MIX), 0.05),
        'w_out': nrm(ks[21], (DEPTH, D_MIX, D_MODEL), D_MIX ** -0.5),
        'norm_ffn': 1.0 + nrm(ks[22], (DEPTH, D_MODEL), 0.05),
        'router_w': nrm(ks[23], (DEPTH, D_MODEL, N_EXPERTS), D_MODEL ** -0.5),
        'router_b': nrm(ks[24], (DEPTH, N_EXPERTS), 0.01),
        'w_gate_up': nrm(ks[25], (DEPTH, N_EXPERTS, D_MODEL, 2 * D_FF), D_MODEL ** -0.5),
        'b_gate_up': nrm(ks[26], (DEPTH, N_EXPERTS, 2 * D_FF), 0.01),
        'w_down': nrm(ks[27], (DEPTH, N_EXPERTS, D_FF, D_MODEL), D_FF ** -0.5),
        'b_down': nrm(ks[28], (DEPTH, N_EXPERTS, D_MODEL), 0.01),
    }


def reference(x, norm_mix, w_in, b_forget, conv_w, conv_b, conv_ln_g, conv_ln_b,
              ssm_A_re, ssm_A_im, ssm_log_dt, ssm_B_re, ssm_B_im, ssm_C_re, ssm_C_im,
              ssm_D, ssm_glu_w, ssm_glu_b, q_norm, k_norm, branch_norm, w_out,
              norm_ffn, router_w, router_b, w_gate_up, b_gate_up, w_down, b_down):
    h = x
    for l in range(DEPTH):
        hn = _rmsnorm(h, norm_mix[l])
        h = h + _hybrid_mixer(hn, w_in[l], b_forget[l], conv_w[l], conv_b[l],
                              conv_ln_g[l], conv_ln_b[l], ssm_A_re[l], ssm_A_im[l],
                              ssm_log_dt[l], ssm_B_re[l], ssm_B_im[l], ssm_C_re[l],
                              ssm_C_im[l], ssm_D[l], ssm_glu_w[l], ssm_glu_b[l],
                              q_norm[l], k_norm[l], branch_norm[l], w_out[l])
        hn = _rmsnorm(h, norm_ffn[l])
        h = h + _moe(hn, router_w[l], router_b[l], w_gate_up[l], b_gate_up[l],
                     w_down[l], b_down[l])
    return h
```

```python
import functools
import math

import jax
import jax.numpy as jnp
import numpy as np
from jax import lax
from jax.experimental import pallas as pl
from jax.experimental.pallas import tpu as pltpu

F32 = jnp.float32
BF16 = jnp.bfloat16

C_CONV = 512
C_SSM = 512
D_ATT = 1024
HEAD_DIM = 128
N_HEADS = 8
CONV_WIDTH = 31
SSM_GROUP_CH = 16
SSM_GROUPS = 32
SSM_STATE = 64
N_EXPERTS = 32
TOP_K = 4
SWIGLU_LIMIT = 7.0
SWIGLU_ALPHA = 1.702
EPS = 1e-6

LANES = 128
VMEM_LIMIT = 56 * 1024 * 1024

Z_CONV = 0
Z_SSM = 2 * C_CONV
Z_Q = Z_SSM + C_SSM
Z_K = Z_Q + D_ATT
Z_V = Z_K + D_ATT
Z_WIDTH = Z_V + D_ATT
IN_TILE = 512

SSM_CHUNK = 16
SSM_LANE_GROUPS = C_SSM // LANES
SSM_GPL = LANES // SSM_GROUP_CH

NEG_BIG = -1e30


def _cparams(sem, vmem=VMEM_LIMIT):
    return pltpu.CompilerParams(dimension_semantics=sem, vmem_limit_bytes=vmem)


def _inproj_kernel(x_ref, g_ref, w_ref, wf_ref, qg_ref, kg_ref, z_ref, f_ref, hn_ref):
    j = pl.program_id(1)

    @pl.when(j == 0)
    def _():
        x = x_ref[...]
        ms = jnp.mean(x * x, axis=-1, keepdims=True)
        hn = (x * lax.rsqrt(ms + EPS) * g_ref[...]).astype(BF16)
        hn_ref[...] = hn
        f_ref[...] = lax.dot_general(wf_ref[...], hn, (((1,), (1,)), ((), ())),
                                     preferred_element_type=F32)

    acc = jnp.dot(hn_ref[...], w_ref[...], preferred_element_type=F32)
    q_lo, k_lo, v_lo = Z_Q // IN_TILE, Z_K // IN_TILE, Z_V // IN_TILE
    is_q = jnp.logical_and(j >= q_lo, j < k_lo)
    is_k = jnp.logical_and(j >= k_lo, j < v_lo)

    def headnorm(gain):
        for hh in range(IN_TILE // HEAD_DIM):
            a = acc[:, hh * HEAD_DIM:(hh + 1) * HEAD_DIM]
            ms = jnp.mean(a * a, axis=-1, keepdims=True)
            z_ref[:, hh * HEAD_DIM:(hh + 1) * HEAD_DIM] = (
                a * lax.rsqrt(ms + EPS) * gain).astype(z_ref.dtype)

    @pl.when(is_q)
    def _():
        headnorm(qg_ref[...])

    @pl.when(is_k)
    def _():
        headnorm(kg_ref[...])

    @pl.when(jnp.logical_not(jnp.logical_or(is_q, is_k)))
    def _():
        z_ref[...] = acc.astype(z_ref.dtype)


def _inproj(h, g, w_main, w_f, q_gain, k_gain, bm=512):
    L, D = h.shape
    n_tiles = Z_WIDTH // IN_TILE
    return pl.pallas_call(
        _inproj_kernel,
        grid=(L // bm, n_tiles),
        in_specs=[
            pl.BlockSpec((bm, D), lambda i, j: (i, 0)),
            pl.BlockSpec((1, D), lambda i, j: (0, 0)),
            pl.BlockSpec((D, IN_TILE), lambda i, j: (0, j)),
            pl.BlockSpec((N_HEADS, D), lambda i, j: (0, 0)),
            pl.BlockSpec((1, HEAD_DIM), lambda i, j: (0, 0)),
            pl.BlockSpec((1, HEAD_DIM), lambda i, j: (0, 0)),
        ],
        out_specs=[
            pl.BlockSpec((bm, IN_TILE), lambda i, j: (i, j)),
            pl.BlockSpec((N_HEADS, bm), lambda i, j: (0, i)),
        ],
        out_shape=[
            jax.ShapeDtypeStruct((L, Z_WIDTH), BF16),
            jax.ShapeDtypeStruct((N_HEADS, L), F32),
        ],
        scratch_shapes=[pltpu.VMEM((bm, D), BF16)],
        compiler_params=_cparams(("parallel", "arbitrary")),
        name="inproj",
    )(h, g, w_main, w_f, q_gain, k_gain)


def _cumgate_kernel(f_ref, b_ref, c_ref):
    x = jax.nn.log_sigmoid(f_ref[...] + b_ref[...])
    n = x.shape[1]
    pos = lax.broadcasted_iota(jnp.int32, x.shape, 1)
    d = 1
    while d < n:
        x = x + jnp.where(pos >= d, pltpu.roll(x, d, axis=1), 0.0)
        d *= 2
    c_ref[...] = x


def _cumgate(f_t, b_forget):
    H, L = f_t.shape
    return pl.pallas_call(
        _cumgate_kernel,
        out_shape=jax.ShapeDtypeStruct((H, L), F32),
        compiler_params=pltpu.CompilerParams(vmem_limit_bytes=VMEM_LIMIT),
        name="cumgate",
    )(f_t, b_forget.reshape(H, 1))


CONV_HALO = 32


def _conv_kernel(a_ref, gate_ref, cw_ref, cb_ref, lg_ref, lb_ref, bg_ref, o_ref, ubuf):
    i = pl.program_id(0)
    bt = a_ref.shape[0]

    @pl.when(i == 0)
    def _():
        ubuf[0:CONV_HALO, :] = jnp.zeros((CONV_HALO, C_CONV), F32)

    @pl.when(i > 0)
    def _():
        ubuf[0:CONV_HALO, :] = ubuf[bt:bt + CONV_HALO, :]

    a = a_ref[...].astype(F32)
    gate = gate_ref[...].astype(F32)
    ubuf[CONV_HALO:CONV_HALO + bt, :] = a * jax.nn.sigmoid(gate)

    base = CONV_HALO - (CONV_WIDTH - 1)
    acc = jnp.zeros((bt, C_CONV), F32)
    for w in range(CONV_WIDTH):
        acc = acc + ubuf[base + w:base + w + bt, :] * cw_ref[w:w + 1, :]
    y = acc + cb_ref[...]
    mu = jnp.mean(y, axis=-1, keepdims=True)
    yc = y - mu
    var = jnp.mean(yc * yc, axis=-1, keepdims=True)
    y = yc * lax.rsqrt(var + EPS) * lg_ref[...] + lb_ref[...]
    y = y * jax.nn.sigmoid(y)
    ms = jnp.mean(y * y, axis=-1, keepdims=True)
    o_ref[...] = (y * lax.rsqrt(ms + EPS) * bg_ref[...]).astype(o_ref.dtype)


def _conv_module(z, conv_w, conv_b, ln_g, ln_b, g_conv, bt=512):
    L = z.shape[0]
    row = lambda v: v.reshape(1, C_CONV)
    vec = pl.BlockSpec((1, C_CONV), lambda i: (0, 0))
    return pl.pallas_call(
        _conv_kernel,
        grid=(L // bt,),
        in_specs=[
            pl.BlockSpec((bt, C_CONV), lambda i: (i, Z_CONV // C_CONV)),
            pl.BlockSpec((bt, C_CONV), lambda i: (i, Z_CONV // C_CONV + 1)),
            pl.BlockSpec((CONV_WIDTH, C_CONV), lambda i: (0, 0)),
            vec, vec, vec, vec,
        ],
        out_specs=pl.BlockSpec((bt, C_CONV), lambda i: (i, 0)),
        out_shape=jax.ShapeDtypeStruct((L, C_CONV), BF16),
        scratch_shapes=[pltpu.VMEM((bt + CONV_HALO, C_CONV), F32)],
        compiler_params=_cparams(("arbitrary",)),
        name="conv_module",
    )(z, z, conv_w, row(conv_b), row(ln_g), row(ln_b), row(g_conv))


def _ssm_operators(A_re, A_im, log_dt, B_re, B_im, C_re, C_im):
    T = SSM_CHUNK
    dt = jnp.exp(log_dt.astype(F32))[:, None]
    A_re = A_re.astype(F32)
    A_im = A_im.astype(F32)
    j = jnp.arange(T + 1, dtype=F32)[:, None, None]
    pw_mag = jnp.exp(j * (dt * A_re)[None])
    pw_ang = j * (dt * A_im)[None]
    p_re, p_im = pw_mag * jnp.cos(pw_ang), pw_mag * jnp.sin(pw_ang)
    ab_re, ab_im = p_re[1], p_im[1]
    inv = 1.0 / (A_re * A_re + A_im * A_im)
    ir, ii = A_re * inv, -A_im * inv
    dr, di = ab_re - 1.0, ab_im
    f_re, f_im = dr * ir - di * ii, dr * ii + di * ir
    Br, Bi = B_re.astype(F32), B_im.astype(F32)
    bb_re = f_re[..., None] * Br - f_im[..., None] * Bi
    bb_im = f_re[..., None] * Bi + f_im[..., None] * Br
    Cr, Ci = C_re.astype(F32), C_im.astype(F32)

    cp_re = Cr[None] * p_re[:, :, None, :] - Ci[None] * p_im[:, :, None, :]
    cp_im = Cr[None] * p_im[:, :, None, :] + Ci[None] * p_re[:, :, None, :]
    kern = (jnp.einsum('jgop,gpi->jgio', cp_re[:T], bb_re)
            - jnp.einsum('jgop,gpi->jgio', cp_im[:T], bb_im))

    n_t, gpl, H, P = SSM_LANE_GROUPS, SSM_GPL, SSM_GROUP_CH, SSM_STATE
    eye = jnp.eye(gpl, dtype=F32)
    s_idx = jnp.arange(T)[:, None]
    t_idx = jnp.arange(T)[None, :]
    lag = jnp.clip(t_idx - s_idx, 0, T - 1)
    causal = (s_idx <= t_idx).astype(F32)
    k_st = kern[lag] * causal[:, :, None, None, None]
    k_st = k_st.reshape(T, T, n_t, gpl, H, H)
    w_toep = jnp.einsum('stngio,gk->nsgitko', k_st, eye).reshape(n_t, T * LANES, T * LANES)

    pe_re, pe_im = p_re[T - 1 - jnp.arange(T)], p_im[T - 1 - jnp.arange(T)]
    e_re = pe_re[..., None] * bb_re[None] - pe_im[..., None] * bb_im[None]
    e_im = pe_re[..., None] * bb_im[None] + pe_im[..., None] * bb_re[None]
    e_ri = jnp.stack([e_re, e_im], axis=0).reshape(2, T, n_t, gpl, P, H)
    w_end = jnp.einsum('rsngpi,gk->nsgirkp', e_ri, eye).reshape(n_t, T * LANES, 2 * gpl * P)

    m_ri = jnp.stack([cp_re[1:], -cp_im[1:]], axis=0).reshape(2, T, n_t, gpl, H, P)
    w_fix = jnp.einsum('rtngop,gk->nrgptko', m_ri, eye).reshape(n_t, 2 * gpl * P, T * LANES)

    a_chunk = jnp.stack([p_re[T], p_im[T]], axis=0).reshape(2, n_t, gpl * P)
    a_chunk = a_chunk.transpose(1, 0, 2).reshape(n_t, 1, 2 * gpl * P)
    return w_toep.astype(BF16), w_end.astype(BF16), w_fix.astype(BF16), a_chunk


def _ssm_scan_kernel(x_ref, wt_ref, we_ref, wf_ref, a_ref, y_ref, carry, sprev, e_scr):
    r = pl.program_id(1)
    rb = x_ref.shape[1]
    ns = carry.shape[1] // 2

    @pl.when(r == 0)
    def _():
        carry[...] = jnp.zeros_like(carry)

    x = x_ref[0]
    y_local = jnp.dot(x, wt_ref[0], preferred_element_type=F32)
    e_scr[...] = jnp.dot(x, we_ref[0], preferred_element_type=F32)
    a_re = a_ref[0, :, 0:ns]
    a_im = a_ref[0, :, ns:2 * ns]

    def step(k, s):
        s_re, s_im = s
        sprev[pl.ds(k, 1), 0:ns] = s_re
        sprev[pl.ds(k, 1), ns:2 * ns] = s_im
        e_re = e_scr[pl.ds(k, 1), 0:ns]
        e_im = e_scr[pl.ds(k, 1), ns:2 * ns]
        return (a_re * s_re - a_im * s_im + e_re, a_re * s_im + a_im * s_re + e_im)

    s_re, s_im = lax.fori_loop(0, rb, step, (carry[:, 0:ns], carry[:, ns:2 * ns]))
    carry[:, 0:ns] = s_re
    carry[:, ns:2 * ns] = s_im
    y_fix = jnp.dot(sprev[...].astype(BF16), wf_ref[0], preferred_element_type=F32)
    y_ref[0] = y_local + y_fix


def _ssm_scan(x3, w_toep, w_end, w_fix, a_chunk, rb=256):
    n_t, n_chunks, width = x3.shape
    ns2 = w_end.shape[2]
    rb = min(rb, n_chunks)
    return pl.pallas_call(
        _ssm_scan_kernel,
        grid=(n_t, n_chunks // rb),
        in_specs=[
            pl.BlockSpec((1, rb, width), lambda n, r: (n, r, 0)),
            pl.BlockSpec((1, width, width), lambda n, r: (n, 0, 0)),
            pl.BlockSpec((1, width, ns2), lambda n, r: (n, 0, 0)),
            pl.BlockSpec((1, ns2, width), lambda n, r: (n, 0, 0)),
            pl.BlockSpec((1, 1, ns2), lambda n, r: (n, 0, 0)),
        ],
        out_specs=pl.BlockSpec((1, rb, width), lambda n, r: (n, r, 0)),
        out_shape=jax.ShapeDtypeStruct((n_t, n_chunks, width), F32),
        scratch_shapes=[
            pltpu.VMEM((1, ns2), F32),
            pltpu.VMEM((rb, ns2), F32),
            pltpu.VMEM((rb, ns2), F32),
        ],
        compiler_params=_cparams(("parallel", "arbitrary")),
        name="ssm_scan",
    )(x3, w_toep, w_end, w_fix, a_chunk)


def _ssm_post_kernel(y_ref, u_ref, d_ref, gw_ref, gb_ref, bg_ref, o_ref):
    y = y_ref[...] + d_ref[...] * u_ref[...].astype(F32)
    y = jax.nn.gelu(y)
    gate = jnp.dot(y.astype(BF16), gw_ref[...], preferred_element_type=F32) + gb_ref[...]
    y = y * jax.nn.sigmoid(gate)
    ms = jnp.mean(y * y, axis=-1, keepdims=True)
    o_ref[...] = (y * lax.rsqrt(ms + EPS) * bg_ref[...]).astype(o_ref.dtype)


def _ssm_post(y_pre, z, ssm_d, glu_w, glu_b, g_ssm, bt=512):
    L = y_pre.shape[0]
    row = lambda v: v.reshape(1, C_SSM)
    vec = pl.BlockSpec((1, C_SSM), lambda i: (0, 0))
    return pl.pallas_call(
        _ssm_post_kernel,
        grid=(L // bt,),
        in_specs=[
            pl.BlockSpec((bt, C_SSM), lambda i: (i, 0)),
            pl.BlockSpec((bt, C_SSM), lambda i: (i, Z_SSM // C_SSM)),
            vec,
            pl.BlockSpec((C_SSM, C_SSM), lambda i: (0, 0)),
            vec, vec,
        ],
        out_specs=pl.BlockSpec((bt, C_SSM), lambda i: (i, 0)),
        out_shape=jax.ShapeDtypeStruct((L, C_SSM), BF16),
        compiler_params=_cparams(("parallel",)),
        name="ssm_post",
    )(y_pre, z, row(ssm_d), glu_w.astype(BF16), row(glu_b), row(g_ssm))


def _s5_module(z, A_re, A_im, log_dt, B_re, B_im, C_re, C_im, ssm_d, glu_w, glu_b, g_ssm):
    L = z.shape[0]
    T = SSM_CHUNK
    ops = _ssm_operators(A_re, A_im, log_dt, B_re, B_im, C_re, C_im)
    u = z[:, Z_SSM:Z_SSM + C_SSM]
    x3 = u.reshape(L // T, T, SSM_LANE_GROUPS, LANES).transpose(2, 0, 1, 3)
    x3 = x3.reshape(SSM_LANE_GROUPS, L // T, T * LANES)
    y3 = _ssm_scan(x3, *ops)
    y_pre = y3.reshape(SSM_LANE_GROUPS, L // T, T, LANES).transpose(1, 2, 0, 3).reshape(L, C_SSM)
    return _ssm_post(y_pre, z, ssm_d, glu_w, glu_b, g_ssm)


def _attn_kernel(qi_ref, kj_ref, q_ref, k_ref, v_ref, cq_ref, ck_ref, o_ref,
                 m_scr, l_scr, acc_scr, cq_scr):
    h = pl.program_id(0)
    p = pl.program_id(1)
    i = qi_ref[p]
    j = kj_ref[p]
    bq, bk = q_ref.shape[0], k_ref.shape[0]

    @pl.when(j == 0)
    def _():
        m_scr[...] = jnp.full_like(m_scr, NEG_BIG)
        l_scr[...] = jnp.zeros_like(l_scr)
        acc_scr[...] = jnp.zeros_like(acc_scr)
        cq = cq_ref[...]
        head = lax.broadcasted_iota(jnp.int32, cq.shape, 1)
        cq_scr[...] = jnp.sum(jnp.where(head == h, cq, 0.0), axis=1, keepdims=True)

    s = lax.dot_general(q_ref[...], k_ref[...], (((1,), (1,)), ((), ())),
                        preferred_element_type=F32)
    s = s + cq_scr[...] - ck_ref[0]

    def update(s):
        m_prev = m_scr[...]
        m_new = jnp.maximum(m_prev, jnp.max(s, axis=1, keepdims=True))
        alpha = jnp.exp(m_prev - m_new)
        pr = jnp.exp(s - m_new)
        l_scr[...] = alpha * l_scr[...] + jnp.sum(pr, axis=1, keepdims=True)
        acc_scr[...] = alpha * acc_scr[...] + jnp.dot(
            pr.astype(BF16), v_ref[...], preferred_element_type=F32)
        m_scr[...] = m_new

    @pl.when(j < i)
    def _():
        update(s)

    @pl.when(j == i)
    def _():
        rows = lax.broadcasted_iota(jnp.int32, (bq, bk), 0)
        cols = lax.broadcasted_iota(jnp.int32, (bq, bk), 1)
        update(jnp.where(cols <= rows, s, NEG_BIG))
        o_ref[...] = (acc_scr[...] / l_scr[...]).astype(o_ref.dtype)


def _fox_attention(z, c, blk=512):
    L = z.shape[0]
    blk = min(blk, L)
    nb = L // blk
    qi = np.repeat(np.arange(nb), np.arange(1, nb + 1)).astype(np.int32)
    kj = np.concatenate([np.arange(i + 1) for i in range(nb)]).astype(np.int32)
    c_col = c.T
    c_row = c.reshape(N_HEADS, 1, L)
    q0, k0, v0 = Z_Q // HEAD_DIM, Z_K // HEAD_DIM, Z_V // HEAD_DIM
    grid_spec = pltpu.PrefetchScalarGridSpec(
        num_scalar_prefetch=2,
        grid=(N_HEADS, len(qi)),
        in_specs=[
            pl.BlockSpec((blk, HEAD_DIM), lambda h, p, qi, kj: (qi[p], q0 + h)),
            pl.BlockSpec((blk, HEAD_DIM), lambda h, p, qi, kj: (kj[p], k0 + h)),
            pl.BlockSpec((blk, HEAD_DIM), lambda h, p, qi, kj: (kj[p], v0 + h)),
            pl.BlockSpec((blk, N_HEADS), lambda h, p, qi, kj: (qi[p], 0)),
            pl.BlockSpec((1, 1, blk), lambda h, p, qi, kj: (h, 0, kj[p])),
        ],
        out_specs=pl.BlockSpec((blk, HEAD_DIM), lambda h, p, qi, kj: (qi[p], h)),
        scratch_shapes=[
            pltpu.VMEM((blk, 1), F32),
            pltpu.VMEM((blk, 1), F32),
            pltpu.VMEM((blk, HEAD_DIM), F32),
            pltpu.VMEM((blk, 1), F32),
        ],
    )
    return pl.pallas_call(
        _attn_kernel,
        grid_spec=grid_spec,
        out_shape=jax.ShapeDtypeStruct((L, D_ATT), BF16),
        compiler_params=_cparams(("parallel", "arbitrary")),
        name="fox_attention",
    )(jnp.asarray(qi), jnp.asarray(kj), z, z, z, c_col, c_row)


def _outproj_kernel(yc_ref, ys_ref, ya_ref, h_ref, wc_ref, ws_ref, wa_ref, ga_ref,
                    nf_ref, rw_ref, rb_ref, ho_ref, hn_ref, idx_ref, gate_ref):
    ya = ya_ref[...].astype(F32)
    ms = jnp.mean(ya * ya, axis=-1, keepdims=True)
    ya = (ya * lax.rsqrt(ms + EPS) * ga_ref[...]).astype(BF16)
    mix = jnp.dot(yc_ref[...], wc_ref[...], preferred_element_type=F32)
    mix = mix + jnp.dot(ys_ref[...], ws_ref[...], preferred_element_type=F32)
    mix = mix + jnp.dot(ya, wa_ref[...], preferred_element_type=F32)
    h = h_ref[...] + mix
    ho_ref[...] = h

    ms = jnp.mean(h * h, axis=-1, keepdims=True)
    hn = h * lax.rsqrt(ms + EPS) * nf_ref[...]
    hn_ref[...] = hn.astype(hn_ref.dtype)
    logits = jnp.dot(hn, rw_ref[...], preferred_element_type=F32,
                     precision=lax.Precision.HIGHEST) + rb_ref[...]
    col = lax.broadcasted_iota(jnp.int32, logits.shape, 1)
    kcol = lax.broadcasted_iota(jnp.int32, idx_ref.shape, 1)
    idx_out = jnp.zeros(idx_ref.shape, jnp.int32)
    val_out = jnp.zeros(gate_ref.shape, F32)
    top = None
    for k in range(TOP_K):
        m = jnp.max(logits, axis=1, keepdims=True)
        sel = jnp.min(jnp.where(logits == m, col, N_EXPERTS), axis=1, keepdims=True)
        if top is None:
            top = m
        idx_out = jnp.where(kcol == k, sel, idx_out)
        val_out = jnp.where(kcol == k, jnp.exp(m - top), val_out)
        logits = jnp.where(col == sel, -jnp.inf, logits)
    idx_ref[...] = idx_out
    gate_ref[...] = val_out / jnp.sum(val_out, axis=1, keepdims=True)


def _outproj_router(yc, ys, ya, h, w_out, g_att, norm_ffn, router_w, router_b, bm=512):
    L, D = h.shape
    full = lambda shape: pl.BlockSpec(shape, lambda i: (0,) * len(shape))
    return pl.pallas_call(
        _outproj_kernel,
        grid=(L // bm,),
        in_specs=[
            pl.BlockSpec((bm, C_CONV), lambda i: (i, 0)),
            pl.BlockSpec((bm, C_SSM), lambda i: (i, 0)),
            pl.BlockSpec((bm, D_ATT), lambda i: (i, 0)),
            pl.BlockSpec((bm, D), lambda i: (i, 0)),
            pl.BlockSpec((C_CONV, D), lambda i: (0, 0)),
            pl.BlockSpec((C_SSM, D), lambda i: (1, 0)),
            pl.BlockSpec((D_ATT, D), lambda i: (1, 0)),
            full((1, D_ATT)),
            full((1, D)),
            full((D, N_EXPERTS)),
            full((1, N_EXPERTS)),
        ],
        out_specs=[
            pl.BlockSpec((bm, D), lambda i: (i, 0)),
            pl.BlockSpec((bm, D), lambda i: (i, 0)),
            pl.BlockSpec((bm, TOP_K), lambda i: (i, 0)),
            pl.BlockSpec((bm, TOP_K), lambda i: (i, 0)),
        ],
        out_shape=[
            jax.ShapeDtypeStruct((L, D), F32),
            jax.ShapeDtypeStruct((L, D), BF16),
            jax.ShapeDtypeStruct((L, TOP_K), jnp.int32),
            jax.ShapeDtypeStruct((L, TOP_K), F32),
        ],
        compiler_params=_cparams(("parallel",)),
        name="outproj_router",
    )(yc, ys, ya, h, w_out, w_out, w_out, g_att.reshape(1, D_ATT),
      norm_ffn.reshape(1, D), router_w, router_b.reshape(1, N_EXPERTS))


GU_GROUP = 2 * LANES


def _gu_perm():
    src = np.arange(GU_GROUP)
    dst = np.where(src % 2 == 0, src // 2, LANES + src // 2)
    p = np.zeros((GU_GROUP, GU_GROUP), np.float32)
    p[src, dst] = 1.0
    return jnp.asarray(p, BF16)


def _wprep_kernel(w_ref, p_ref, o_ref):
    perm = p_ref[...]
    for c in range(w_ref.shape[2] // GU_GROUP):
        blk = w_ref[0, :, c * GU_GROUP:(c + 1) * GU_GROUP].astype(BF16)
        o_ref[0, :, c * GU_GROUP:(c + 1) * GU_GROUP] = jnp.dot(
            blk, perm, preferred_element_type=F32).astype(BF16)


def _prep_gate_up(w_gu, br=1024, bc=1024):
    E, D, N = w_gu.shape
    return pl.pallas_call(
        _wprep_kernel,
        grid=(E, D // br, N // bc),
        in_specs=[
            pl.BlockSpec((1, br, bc), lambda e, i, j: (e, i, j)),
            pl.BlockSpec((GU_GROUP, GU_GROUP), lambda e, i, j: (0, 0)),
        ],
        out_specs=pl.BlockSpec((1, br, bc), lambda e, i, j: (e, i, j)),
        out_shape=jax.ShapeDtypeStruct((E, D, N), BF16),
        compiler_params=_cparams(("parallel", "parallel", "parallel")),
        name="prep_gate_up",
    )(w_gu, _gu_perm())


def _moe_kernel(be_ref, nu_ref, x_ref, wgu_ref, bgu_ref, wdn_ref, bdn_ref, g_ref, o_ref, acc_ref):
    b = pl.program_id(0)
    j = pl.program_id(1)
    nj = pl.num_programs(1)

    @pl.when(b < nu_ref[0])
    def _():
        @pl.when(j == 0)
        def _():
            acc_ref[...] = jnp.zeros_like(acc_ref)

        hgu = jnp.dot(x_ref[...], wgu_ref[0], preferred_element_type=F32) + bgu_ref[0]
        parts = []
        for c in range(hgu.shape[1] // GU_GROUP):
            g = jnp.minimum(hgu[:, c * GU_GROUP:c * GU_GROUP + LANES], SWIGLU_LIMIT)
            u = jnp.clip(hgu[:, c * GU_GROUP + LANES:(c + 1) * GU_GROUP],
                         -SWIGLU_LIMIT, SWIGLU_LIMIT)
            parts.append((g * jax.nn.sigmoid(SWIGLU_ALPHA * g) * (u + 1.0)).astype(BF16))
        a = jnp.concatenate(parts, axis=1)
        acc_ref[...] += jnp.dot(a, wdn_ref[0].astype(BF16), preferred_element_type=F32)

        @pl.when(j == nj - 1)
        def _():
            o_ref[...] = ((acc_ref[...] + bdn_ref[0]) * g_ref[...]).astype(o_ref.dtype)


def _moe_experts(xs, block_e, n_used, wgu_p, bgu_p, w_dn, b_dn, row_gate, bm, tf=512):
    R, D = xs.shape
    E, d_ff, _ = w_dn.shape
    nb = R // bm
    nj = d_ff // tf

    def blk(b, nu):
        return jnp.minimum(b, nu[0] - 1)

    grid_spec = pltpu.PrefetchScalarGridSpec(
        num_scalar_prefetch=2,
        grid=(nb, nj),
        in_specs=[
            pl.BlockSpec((bm, D), lambda b, j, be, nu: (blk(b, nu), 0)),
            pl.BlockSpec((1, D, 2 * tf), lambda b, j, be, nu: (be[blk(b, nu)], 0, j)),
            pl.BlockSpec((1, 1, 2 * tf), lambda b, j, be, nu: (be[blk(b, nu)], 0, j)),
            pl.BlockSpec((1, tf, D), lambda b, j, be, nu: (be[blk(b, nu)], j, 0)),
            pl.BlockSpec((1, 1, D), lambda b, j, be, nu: (be[blk(b, nu)], 0, 0)),
            pl.BlockSpec((bm, 1), lambda b, j, be, nu: (blk(b, nu), 0)),
        ],
        out_specs=pl.BlockSpec((bm, D), lambda b, j, be, nu: (blk(b, nu), 0)),
        scratch_shapes=[pltpu.VMEM((bm, D), F32)],
    )
    return pl.pallas_call(
        _moe_kernel,
        grid_spec=grid_spec,
        out_shape=jax.ShapeDtypeStruct((R, D), F32),
        compiler_params=_cparams(("arbitrary", "arbitrary")),
        name="moe_experts",
    )(block_e, n_used, xs, wgu_p, bgu_p.reshape(E, 1, 2 * d_ff), w_dn,
      b_dn.reshape(E, 1, D), row_gate)


def _moe(h, hn, top_idx, gates, w_gu, b_gu, w_dn, b_dn, bm=512):
    T, D = h.shape
    E = N_EXPERTS
    n_rows = T * TOP_K
    flat_e = top_idx.reshape(-1)
    flat_g = gates.reshape(-1)
    order = jnp.argsort(flat_e, stable=True)
    s_e = flat_e[order]
    counts = jnp.bincount(flat_e, length=E)
    padded = (counts + bm - 1) // bm * bm
    start = jnp.cumsum(counts) - counts
    pend = jnp.cumsum(padded)
    pstart = pend - padded
    dest = (pstart[s_e] + (jnp.arange(n_rows) - start[s_e])).astype(jnp.int32)
    n_blocks = -(-n_rows // bm) + E
    R = n_blocks * bm
    buf_tok = jnp.zeros((R,), jnp.int32).at[dest].set((order // TOP_K).astype(jnp.int32))
    buf_g = jnp.zeros((R,), F32).at[dest].set(flat_g[order])
    block_e = jnp.minimum(
        jnp.searchsorted(pend, jnp.arange(n_blocks) * bm, side='right'), E - 1).astype(jnp.int32)
    n_used = (pend[-1:] // bm).astype(jnp.int32)
    pos = jnp.zeros((n_rows,), jnp.int32).at[order].set(dest).reshape(T, TOP_K)

    wgu_p = _prep_gate_up(w_gu)
    bgu_p = b_gu.reshape(E, -1, LANES, 2).transpose(0, 1, 3, 2).reshape(E, -1)
    xs = hn[buf_tok]
    ys = _moe_experts(xs, block_e, n_used, wgu_p, bgu_p, w_dn, b_dn, buf_g.reshape(R, 1), bm)
    return h + jnp.sum(ys[pos], axis=1)


def _layer(h, p):
    w_in = p['w_in']
    scale = HEAD_DIM ** -0.5
    z, f_t = _inproj(
        h, p['norm_mix'].reshape(1, -1),
        w_in[:, :Z_WIDTH].astype(BF16),
        w_in[:, Z_WIDTH:].T.astype(BF16),
        (p['q_norm'] * scale).reshape(1, HEAD_DIM), p['k_norm'].reshape(1, HEAD_DIM))
    c = _cumgate(f_t, p['b_forget'])
    g_conv = p['branch_norm'][:C_CONV]
    g_ssm = p['branch_norm'][C_CONV:C_CONV + C_SSM]
    g_att = p['branch_norm'][C_CONV + C_SSM:]
    yc = _conv_module(z, p['conv_w'], p['conv_b'], p['conv_ln_g'], p['conv_ln_b'], g_conv)
    ys = _s5_module(z, p['ssm_A_re'], p['ssm_A_im'], p['ssm_log_dt'], p['ssm_B_re'],
                    p['ssm_B_im'], p['ssm_C_re'], p['ssm_C_im'], p['ssm_D'],
                    p['ssm_glu_w'], p['ssm_glu_b'], g_ssm)
    ya = _fox_attention(z, c)
    h, hn, top_idx, gates = _outproj_router(
        yc, ys, ya, h, p['w_out'].astype(BF16), g_att, p['norm_ffn'],
        p['router_w'], p['router_b'])
    return _moe(h, hn, top_idx, gates, p['w_gate_up'], p['b_gate_up'], p['w_down'], p['b_down'])


def kernel(x, norm_mix, w_in, b_forget, conv_w, conv_b, conv_ln_g, conv_ln_b, ssm_A_re,
           ssm_A_im, ssm_log_dt, ssm_B_re, ssm_B_im, ssm_C_re, ssm_C_im, ssm_D, ssm_glu_w,
           ssm_glu_b, q_norm, k_norm, branch_norm, w_out, norm_ffn, router_w, router_b,
           w_gate_up, b_gate_up, w_down, b_down):
    params = dict(
        norm_mix=norm_mix, w_in=w_in, b_forget=b_forget, conv_w=conv_w, conv_b=conv_b,
        conv_ln_g=conv_ln_g, conv_ln_b=conv_ln_b, ssm_A_re=ssm_A_re, ssm_A_im=ssm_A_im,
        ssm_log_dt=ssm_log_dt, ssm_B_re=ssm_B_re, ssm_B_im=ssm_B_im, ssm_C_re=ssm_C_re,
        ssm_C_im=ssm_C_im, ssm_D=ssm_D, ssm_glu_w=ssm_glu_w, ssm_glu_b=ssm_glu_b,
        q_norm=q_norm, k_norm=k_norm, branch_norm=branch_norm, w_out=w_out,
        norm_ffn=norm_ffn, router_w=router_w, router_b=router_b, w_gate_up=w_gate_up,
        b_gate_up=b_gate_up, w_down=w_down, b_down=b_down)
    bsz, L, D = x.shape
    assert bsz == 1
    h = x.reshape(L, D)
    for l in range(norm_mix.shape[0]):
        h = _layer(h, {k: v[l] for k, v in params.items()})
    return h.reshape(bsz, L, D)
```

```python
import functools
import math

import jax
import jax.numpy as jnp
import numpy as np
from jax import lax
from jax.experimental import pallas as pl
from jax.experimental.pallas import tpu as pltpu

F32 = jnp.float32
BF16 = jnp.bfloat16

C_CONV = 512
C_SSM = 512
D_ATT = 1024
HEAD_DIM = 128
N_HEADS = 8
CONV_WIDTH = 31
SSM_GROUP_CH = 16
SSM_GROUPS = 32
SSM_STATE = 64
N_EXPERTS = 32
TOP_K = 4
SWIGLU_LIMIT = 7.0
SWIGLU_ALPHA = 1.702
EPS = 1e-6

LANES = 128
VMEM_LIMIT = 56 * 1024 * 1024

Z_CONV = 0
Z_SSM = 2 * C_CONV
Z_Q = Z_SSM + C_SSM
Z_K = Z_Q + D_ATT
Z_V = Z_K + D_ATT
Z_WIDTH = Z_V + D_ATT
IN_TILE = 512

SSM_CHUNK = 16
SSM_LANE_GROUPS = C_SSM // LANES
SSM_GPL = LANES // SSM_GROUP_CH

NEG_BIG = -1e30


def _cparams(sem, vmem=VMEM_LIMIT):
    return pltpu.CompilerParams(dimension_semantics=sem, vmem_limit_bytes=vmem)


def _inproj_kernel(x_ref, g_ref, w_ref, wf_ref, qg_ref, kg_ref, z_ref, f_ref, hn_ref):
    j = pl.program_id(1)

    @pl.when(j == 0)
    def _():
        x = x_ref[...]
        ms = jnp.mean(x * x, axis=-1, keepdims=True)
        hn = (x * lax.rsqrt(ms + EPS) * g_ref[...]).astype(BF16)
        hn_ref[...] = hn
        f_ref[...] = lax.dot_general(wf_ref[...], hn, (((1,), (1,)), ((), ())),
                                     preferred_element_type=F32)

    acc = jnp.dot(hn_ref[...], w_ref[...], preferred_element_type=F32)
    q_lo, k_lo, v_lo = Z_Q // IN_TILE, Z_K // IN_TILE, Z_V // IN_TILE
    is_q = jnp.logical_and(j >= q_lo, j < k_lo)
    is_k = jnp.logical_and(j >= k_lo, j < v_lo)

    def headnorm(gain):
        for hh in range(IN_TILE // HEAD_DIM):
            a = acc[:, hh * HEAD_DIM:(hh + 1) * HEAD_DIM]
            ms = jnp.mean(a * a, axis=-1, keepdims=True)
            z_ref[:, hh * HEAD_DIM:(hh + 1) * HEAD_DIM] = (
                a * lax.rsqrt(ms + EPS) * gain).astype(z_ref.dtype)

    @pl.when(is_q)
    def _():
        headnorm(qg_ref[...])

    @pl.when(is_k)
    def _():
        headnorm(kg_ref[...])

    @pl.when(jnp.logical_not(jnp.logical_or(is_q, is_k)))
    def _():
        z_ref[...] = acc.astype(z_ref.dtype)


def _inproj(h, g, w_main, w_f, q_gain, k_gain, bm=512):
    L, D = h.shape
    n_tiles = Z_WIDTH // IN_TILE
    return pl.pallas_call(
        _inproj_kernel,
        grid=(L // bm, n_tiles),
        in_specs=[
            pl.BlockSpec((bm, D), lambda i, j: (i, 0)),
            pl.BlockSpec((1, D), lambda i, j: (0, 0)),
            pl.BlockSpec((D, IN_TILE), lambda i, j: (0, j)),
            pl.BlockSpec((N_HEADS, D), lambda i, j: (0, 0)),
            pl.BlockSpec((1, HEAD_DIM), lambda i, j: (0, 0)),
            pl.BlockSpec((1, HEAD_DIM), lambda i, j: (0, 0)),
        ],
        out_specs=[
            pl.BlockSpec((bm, IN_TILE), lambda i, j: (i, j)),
            pl.BlockSpec((N_HEADS, bm), lambda i, j: (0, i)),
        ],
        out_shape=[
            jax.ShapeDtypeStruct((L, Z_WIDTH), BF16),
            jax.ShapeDtypeStruct((N_HEADS, L), F32),
        ],
        scratch_shapes=[pltpu.VMEM((bm, D), BF16)],
        compiler_params=_cparams(("parallel", "arbitrary")),
        name="inproj",
    )(h, g, w_main, w_f, q_gain, k_gain)


def _cumgate_kernel(f_ref, b_ref, c_ref):
    x = jax.nn.log_sigmoid(f_ref[...] + b_ref[...])
    n = x.shape[1]
    pos = lax.broadcasted_iota(jnp.int32, x.shape, 1)
    d = 1
    while d < n:
        x = x + jnp.where(pos >= d, pltpu.roll(x, d, axis=1), 0.0)
        d *= 2
    c_ref[...] = x


def _cumgate(f_t, b_forget):
    H, L = f_t.shape
    return pl.pallas_call(
        _cumgate_kernel,
        out_shape=jax.ShapeDtypeStruct((H, L), F32),
        compiler_params=pltpu.CompilerParams(vmem_limit_bytes=VMEM_LIMIT),
        name="cumgate",
    )(f_t, b_forget.reshape(H, 1))


CONV_HALO = 32


def _conv_kernel(a_ref, gate_ref, cw_ref, cb_ref, lg_ref, lb_ref, bg_ref, o_ref, ubuf):
    i = pl.program_id(0)
    bt = a_ref.shape[0]

    @pl.when(i == 0)
    def _():
        ubuf[0:CONV_HALO, :] = jnp.zeros((CONV_HALO, C_CONV), F32)

    @pl.when(i > 0)
    def _():
        ubuf[0:CONV_HALO, :] = ubuf[bt:bt + CONV_HALO, :]

    a = a_ref[...].astype(F32)
    gate = gate_ref[...].astype(F32)
    ubuf[CONV_HALO:CONV_HALO + bt, :] = a * jax.nn.sigmoid(gate)

    base = CONV_HALO - (CONV_WIDTH - 1)
    acc = jnp.zeros((bt, C_CONV), F32)
    for w in range(CONV_WIDTH):
        acc = acc + ubuf[base + w:base + w + bt, :] * cw_ref[w:w + 1, :]
    y = acc + cb_ref[...]
    mu = jnp.mean(y, axis=-1, keepdims=True)
    yc = y - mu
    var = jnp.mean(yc * yc, axis=-1, keepdims=True)
    y = yc * lax.rsqrt(var + EPS) * lg_ref[...] + lb_ref[...]
    y = y * jax.nn.sigmoid(y)
    ms = jnp.mean(y * y, axis=-1, keepdims=True)
    o_ref[...] = (y * lax.rsqrt(ms + EPS) * bg_ref[...]).astype(o_ref.dtype)


def _conv_module(z, conv_w, conv_b, ln_g, ln_b, g_conv, bt=512):
    L = z.shape[0]
    row = lambda v: v.reshape(1, C_CONV)
    vec = pl.BlockSpec((1, C_CONV), lambda i: (0, 0))
    return pl.pallas_call(
        _conv_kernel,
        grid=(L // bt,),
        in_specs=[
            pl.BlockSpec((bt, C_CONV), lambda i: (i, Z_CONV // C_CONV)),
            pl.BlockSpec((bt, C_CONV), lambda i: (i, Z_CONV // C_CONV + 1)),
            pl.BlockSpec((CONV_WIDTH, C_CONV), lambda i: (0, 0)),
            vec, vec, vec, vec,
        ],
        out_specs=pl.BlockSpec((bt, C_CONV), lambda i: (i, 0)),
        out_shape=jax.ShapeDtypeStruct((L, C_CONV), BF16),
        scratch_shapes=[pltpu.VMEM((bt + CONV_HALO, C_CONV), F32)],
        compiler_params=_cparams(("arbitrary",)),
        name="conv_module",
    )(z, z, conv_w, row(conv_b), row(ln_g), row(ln_b), row(g_conv))


def _ssm_prep_kernel(par_ref, btr_ref, bti_ref, ctr_ref, cti_ref, we_ref, wf_ref, bk_ref, ac_ref):
    T = SSM_CHUNK
    a_re = par_ref[0, 0:1, :]
    a_im = par_ref[0, 1:2, :]
    dt = jnp.exp(par_ref[0, 2:3, :])
    shape = btr_ref.shape[1:]
    same_group = (lax.broadcasted_iota(jnp.int32, shape, 0) // SSM_GROUP_CH
                  == lax.broadcasted_iota(jnp.int32, shape, 1) // SSM_STATE)

    def power(j):
        mag = jnp.exp(float(j) * (dt * a_re))
        ang = float(j) * (dt * a_im)
        return mag * jnp.cos(ang), mag * jnp.sin(ang)

    ab_re, ab_im = power(1)
    inv = 1.0 / (a_re * a_re + a_im * a_im)
    ir, ii = a_re * inv, -a_im * inv
    dr, di = ab_re - 1.0, ab_im
    f_re, f_im = dr * ir - di * ii, dr * ii + di * ir
    btr, bti = btr_ref[0], bti_ref[0]
    bb_re = jnp.where(same_group, f_re * btr - f_im * bti, 0.0)
    bb_im = jnp.where(same_group, f_re * bti + f_im * btr, 0.0)
    c_re = jnp.where(same_group, ctr_ref[0], 0.0)
    c_im = jnp.where(same_group, cti_ref[0], 0.0)
    c0 = jnp.concatenate([c_re, -c_im], axis=1)

    for j in range(T + 1):
        p_re, p_im = power(j)
        if j < T:
            ebar = jnp.concatenate([p_re * bb_re - p_im * bb_im,
                                    p_re * bb_im + p_im * bb_re], axis=1)
            we_ref[0, T - 1 - j] = ebar.astype(we_ref.dtype)
            bk_ref[0, j] = lax.dot_general(
                ebar, c0, (((1,), (1,)), ((), ())), preferred_element_type=F32,
                precision=lax.Precision.HIGHEST).astype(bk_ref.dtype)
        if j >= 1:
            wf_ref[0, j - 1] = jnp.concatenate(
                [c_re * p_re - c_im * p_im, -(c_re * p_im + c_im * p_re)],
                axis=1).astype(wf_ref.dtype)
        if j == T:
            ac_ref[0] = jnp.concatenate([p_re, p_im], axis=1)


def _ssm_prep(A_re, A_im, log_dt, B_re, B_im, C_re, C_im):
    n_t, gpl, H, P, T = SSM_LANE_GROUPS, SSM_GPL, SSM_GROUP_CH, SSM_STATE, SSM_CHUNK
    ns = gpl * P
    par = jnp.stack([A_re.reshape(n_t, ns), A_im.reshape(n_t, ns),
                     jnp.repeat(log_dt, P).reshape(n_t, ns)], axis=1).astype(F32)

    def b_tile(b):
        bt = b.astype(F32).reshape(n_t, gpl, P, H).transpose(0, 1, 3, 2).reshape(n_t, LANES, P)
        return jnp.tile(bt, (1, 1, gpl))

    def c_tile(c):
        return jnp.tile(c.astype(F32).reshape(n_t, LANES, P), (1, 1, gpl))

    tile_spec = pl.BlockSpec((1, LANES, ns), lambda n: (n, 0, 0))
    op_spec = pl.BlockSpec((1, T, LANES, 2 * ns), lambda n: (n, 0, 0, 0))
    w_end, w_fix_t, bker, a_chunk = pl.pallas_call(
        _ssm_prep_kernel,
        grid=(n_t,),
        in_specs=[pl.BlockSpec((1, 3, ns), lambda n: (n, 0, 0)),
                  tile_spec, tile_spec, tile_spec, tile_spec],
        out_specs=[op_spec, op_spec,
                   pl.BlockSpec((1, T, LANES, LANES), lambda n: (n, 0, 0, 0)),
                   pl.BlockSpec((1, 1, 2 * ns), lambda n: (n, 0, 0))],
        out_shape=[jax.ShapeDtypeStruct((n_t, T, LANES, 2 * ns), BF16),
                   jax.ShapeDtypeStruct((n_t, T, LANES, 2 * ns), BF16),
                   jax.ShapeDtypeStruct((n_t, T, LANES, LANES), BF16),
                   jax.ShapeDtypeStruct((n_t, 1, 2 * ns), F32)],
        compiler_params=_cparams(("parallel",)),
        name="ssm_prep",
    )(par, b_tile(B_re), b_tile(B_im), c_tile(C_re), c_tile(C_im))
    return (bker, w_end.reshape(n_t, T * LANES, 2 * ns),
            w_fix_t.reshape(n_t, T * LANES, 2 * ns), a_chunk)


def _ssm_scan_kernel(x_ref, bk_ref, we_ref, wf_ref, a_ref, y_ref, wt_scr, carry, sprev, e_scr):
    r = pl.program_id(1)
    rb = x_ref.shape[1]
    ns = carry.shape[1] // 2
    T = SSM_CHUNK

    @pl.when(r == 0)
    def _():
        carry[...] = jnp.zeros_like(carry)
        for s in range(T):
            for t in range(T):
                blk = bk_ref[0, t - s] if t >= s else jnp.zeros((LANES, LANES), wt_scr.dtype)
                wt_scr[s * LANES:(s + 1) * LANES, t * LANES:(t + 1) * LANES] = blk

    x = x_ref[0]
    y_local = jnp.dot(x, wt_scr[...], preferred_element_type=F32)
    e_scr[...] = jnp.dot(x, we_ref[0], preferred_element_type=F32)
    a_re = a_ref[0, :, 0:ns]
    a_im = a_ref[0, :, ns:2 * ns]

    def step(k, s):
        s_re, s_im = s
        sprev[pl.ds(k, 1), 0:ns] = s_re
        sprev[pl.ds(k, 1), ns:2 * ns] = s_im
        e_re = e_scr[pl.ds(k, 1), 0:ns]
        e_im = e_scr[pl.ds(k, 1), ns:2 * ns]
        return (a_re * s_re - a_im * s_im + e_re, a_re * s_im + a_im * s_re + e_im)

    s_re, s_im = lax.fori_loop(0, rb, step, (carry[:, 0:ns], carry[:, ns:2 * ns]))
    carry[:, 0:ns] = s_re
    carry[:, ns:2 * ns] = s_im
    y_fix = lax.dot_general(sprev[...].astype(BF16), wf_ref[0], (((1,), (1,)), ((), ())),
                            preferred_element_type=F32)
    y_ref[0] = y_local + y_fix


def _ssm_scan(x3, bker, w_end, w_fix_t, a_chunk, rb=256):
    n_t, n_chunks, width = x3.shape
    ns2 = w_end.shape[2]
    rb = min(rb, n_chunks)
    return pl.pallas_call(
        _ssm_scan_kernel,
        grid=(n_t, n_chunks // rb),
        in_specs=[
            pl.BlockSpec((1, rb, width), lambda n, r: (n, r, 0)),
            pl.BlockSpec((1, SSM_CHUNK, LANES, LANES), lambda n, r: (n, 0, 0, 0)),
            pl.BlockSpec((1, width, ns2), lambda n, r: (n, 0, 0)),
            pl.BlockSpec((1, width, ns2), lambda n, r: (n, 0, 0)),
            pl.BlockSpec((1, 1, ns2), lambda n, r: (n, 0, 0)),
        ],
        out_specs=pl.BlockSpec((1, rb, width), lambda n, r: (n, r, 0)),
        out_shape=jax.ShapeDtypeStruct((n_t, n_chunks, width), F32),
        scratch_shapes=[
            pltpu.VMEM((width, width), BF16),
            pltpu.VMEM((1, ns2), F32),
            pltpu.VMEM((rb, ns2), F32),
            pltpu.VMEM((rb, ns2), F32),
        ],
        compiler_params=_cparams(("parallel", "arbitrary")),
        name="ssm_scan",
    )(x3, bker, w_end, w_fix_t, a_chunk)


def _ssm_post_kernel(y_ref, u_ref, d_ref, gw_ref, gb_ref, bg_ref, o_ref):
    y = y_ref[...] + d_ref[...] * u_ref[...].astype(F32)
    y = jax.nn.gelu(y)
    gate = jnp.dot(y.astype(BF16), gw_ref[...], preferred_element_type=F32) + gb_ref[...]
    y = y * jax.nn.sigmoid(gate)
    ms = jnp.mean(y * y, axis=-1, keepdims=True)
    o_ref[...] = (y * lax.rsqrt(ms + EPS) * bg_ref[...]).astype(o_ref.dtype)


def _ssm_post(y_pre, z, ssm_d, glu_w, glu_b, g_ssm, bt=512):
    L = y_pre.shape[0]
    row = lambda v: v.reshape(1, C_SSM)
    vec = pl.BlockSpec((1, C_SSM), lambda i: (0, 0))
    return pl.pallas_call(
        _ssm_post_kernel,
        grid=(L // bt,),
        in_specs=[
            pl.BlockSpec((bt, C_SSM), lambda i: (i, 0)),
            pl.BlockSpec((bt, C_SSM), lambda i: (i, Z_SSM // C_SSM)),
            vec,
            pl.BlockSpec((C_SSM, C_SSM), lambda i: (0, 0)),
            vec, vec,
        ],
        out_specs=pl.BlockSpec((bt, C_SSM), lambda i: (i, 0)),
        out_shape=jax.ShapeDtypeStruct((L, C_SSM), BF16),
        compiler_params=_cparams(("parallel",)),
        name="ssm_post",
    )(y_pre, z, row(ssm_d), glu_w.astype(BF16), row(glu_b), row(g_ssm))


def _s5_module(z, A_re, A_im, log_dt, B_re, B_im, C_re, C_im, ssm_d, glu_w, glu_b, g_ssm):
    L = z.shape[0]
    T = SSM_CHUNK
    ops = _ssm_prep(A_re, A_im, log_dt, B_re, B_im, C_re, C_im)
    u = z[:, Z_SSM:Z_SSM + C_SSM]
    x3 = u.reshape(L // T, T, SSM_LANE_GROUPS, LANES).transpose(2, 0, 1, 3)
    x3 = x3.reshape(SSM_LANE_GROUPS, L // T, T * LANES)
    y3 = _ssm_scan(x3, *ops)
    y_pre = y3.reshape(SSM_LANE_GROUPS, L // T, T, LANES).transpose(1, 2, 0, 3).reshape(L, C_SSM)
    return _ssm_post(y_pre, z, ssm_d, glu_w, glu_b, g_ssm)


BIAS_COLS = 6
LOG2E = 1.4426950408889634


def _attn_kernel(qi_ref, kj_ref, ref_ref, q_ref, aq_ref, k_ref, ak_ref, v_ref, o_ref,
                 qq_scr, m_scr, acc_scr):
    h = pl.program_id(0)
    p = pl.program_id(1)
    i = qi_ref[p]
    j = kj_ref[p]
    bq, bk = q_ref.shape[0], k_ref.shape[0]

    @pl.when(j == 0)
    def _():
        qq_scr[:, 0:HEAD_DIM] = q_ref[...]
        qq_scr[:, HEAD_DIM:2 * HEAD_DIM] = aq_ref[...]
        m_scr[...] = jnp.full_like(m_scr, NEG_BIG)
        acc_scr[...] = jnp.zeros_like(acc_scr)

    lane = lax.broadcasted_iota(jnp.int32, (bk, HEAD_DIM), 1)
    mine = jnp.logical_and(lane >= BIAS_COLS * h, lane < BIAS_COLS * (h + 1))
    ak = jnp.where(mine, ak_ref[...], jnp.zeros_like(ak_ref[...]))
    kk = jnp.concatenate([k_ref[...], ak], axis=1)
    ones_col = jnp.where(lane == 0, 1.0, 0.0).astype(BF16)
    vv = jnp.concatenate([v_ref[...], ones_col], axis=1)
    s = lax.dot_general(qq_scr[...], kk, (((1,), (1,)), ((), ())),
                        preferred_element_type=F32)
    n_blk = ref_ref.shape[0] // N_HEADS
    shift = ref_ref[h * n_blk + i] - ref_ref[h * n_blk + j]

    def update(s):
        m_prev = m_scr[...]
        m_new = jnp.maximum(m_prev, jnp.max(s, axis=1, keepdims=True) + shift)
        alpha = jnp.exp2(m_prev - m_new)
        pr = jnp.exp2(s - (m_new - shift)).astype(BF16)
        acc_scr[...] = alpha * acc_scr[...] + jnp.dot(pr, vv, preferred_element_type=F32)
        m_scr[...] = m_new

    @pl.when(j < i)
    def _():
        update(s)

    @pl.when(j == i)
    def _():
        rows = lax.broadcasted_iota(jnp.int32, (bq, bk), 0)
        cols = lax.broadcasted_iota(jnp.int32, (bq, bk), 1)
        update(jnp.where(cols <= rows, s, NEG_BIG))
        acc = acc_scr[...]
        o_ref[...] = (acc[:, 0:HEAD_DIM] / acc[:, HEAD_DIM:HEAD_DIM + 1]).astype(o_ref.dtype)


def _fox_attention(z, c, blk=2048):
    L = z.shape[0]
    blk = min(blk, L)
    nb = L // blk
    qi = np.repeat(np.arange(nb), np.arange(1, nb + 1)).astype(np.int32)
    kj = np.concatenate([np.arange(i + 1) for i in range(nb)]).astype(np.int32)

    c2 = (c * LOG2E).reshape(N_HEADS, nb, blk)
    ref = c2[:, :, 0]
    d = (c2 - ref[:, :, None]).reshape(N_HEADS, L).T

    def bf16_part(v):
        bits = lax.bitcast_convert_type(v, jnp.uint32) & jnp.uint32(0xFFFF0000)
        return lax.bitcast_convert_type(bits, F32)

    p1 = bf16_part(d)
    p2 = bf16_part(d - p1)
    p3 = bf16_part(d - p1 - p2)
    d1, d2, d3 = p1.astype(BF16), p2.astype(BF16), p3.astype(BF16)
    one = jnp.ones_like(d1)
    pad = jnp.zeros((L, HEAD_DIM - N_HEADS * BIAS_COLS), BF16)
    aq = jnp.concatenate(
        [jnp.stack([d1, d2, d3, one, one, one], axis=-1).reshape(L, -1), pad], axis=1)
    ak = jnp.concatenate(
        [jnp.stack([one, one, one, -d1, -d2, -d3], axis=-1).reshape(L, -1), pad], axis=1)

    q0, k0, v0 = Z_Q // HEAD_DIM, Z_K // HEAD_DIM, Z_V // HEAD_DIM
    grid_spec = pltpu.PrefetchScalarGridSpec(
        num_scalar_prefetch=3,
        grid=(N_HEADS, len(qi)),
        in_specs=[
            pl.BlockSpec((blk, HEAD_DIM), lambda h, p, qi, kj, rf: (qi[p], q0 + h)),
            pl.BlockSpec((blk, HEAD_DIM), lambda h, p, qi, kj, rf: (qi[p], 0)),
            pl.BlockSpec((blk, HEAD_DIM), lambda h, p, qi, kj, rf: (kj[p], k0 + h)),
            pl.BlockSpec((blk, HEAD_DIM), lambda h, p, qi, kj, rf: (kj[p], 0)),
            pl.BlockSpec((blk, HEAD_DIM), lambda h, p, qi, kj, rf: (kj[p], v0 + h)),
        ],
        out_specs=pl.BlockSpec((blk, HEAD_DIM), lambda h, p, qi, kj, rf: (qi[p], h)),
        scratch_shapes=[
            pltpu.VMEM((blk, 2 * HEAD_DIM), BF16),
            pltpu.VMEM((blk, 1), F32),
            pltpu.VMEM((blk, 2 * HEAD_DIM), F32),
        ],
    )
    return pl.pallas_call(
        _attn_kernel,
        grid_spec=grid_spec,
        out_shape=jax.ShapeDtypeStruct((L, D_ATT), BF16),
        compiler_params=_cparams(("parallel", "arbitrary")),
        name="fox_attention",
    )(jnp.asarray(qi), jnp.asarray(kj), ref.reshape(-1), z, aq, z, ak, z)


def _outproj_kernel(yc_ref, ys_ref, ya_ref, h_ref, wc_ref, ws_ref, wa_ref, ga_ref,
                    nf_ref, rw_ref, rb_ref, ho_ref, hn_ref, idx_ref, gate_ref):
    ya = ya_ref[...].astype(F32)
    ms = jnp.mean(ya * ya, axis=-1, keepdims=True)
    ya = (ya * lax.rsqrt(ms + EPS) * ga_ref[...]).astype(BF16)
    mix = jnp.dot(yc_ref[...], wc_ref[...], preferred_element_type=F32)
    mix = mix + jnp.dot(ys_ref[...], ws_ref[...], preferred_element_type=F32)
    mix = mix + jnp.dot(ya, wa_ref[...], preferred_element_type=F32)
    h = h_ref[...] + mix
    ho_ref[...] = h

    ms = jnp.mean(h * h, axis=-1, keepdims=True)
    hn = h * lax.rsqrt(ms + EPS) * nf_ref[...]
    hn_ref[...] = hn.astype(hn_ref.dtype)
    logits = jnp.dot(hn, rw_ref[...], preferred_element_type=F32,
                     precision=lax.Precision.HIGHEST) + rb_ref[...]
    col = lax.broadcasted_iota(jnp.int32, logits.shape, 1)
    kcol = lax.broadcasted_iota(jnp.int32, idx_ref.shape, 1)
    idx_out = jnp.zeros(idx_ref.shape, jnp.int32)
    val_out = jnp.zeros(gate_ref.shape, F32)
    top = None
    for k in range(TOP_K):
        m = jnp.max(logits, axis=1, keepdims=True)
        sel = jnp.min(jnp.where(logits == m, col, N_EXPERTS), axis=1, keepdims=True)
        if top is None:
            top = m
        idx_out = jnp.where(kcol == k, sel, idx_out)
        val_out = jnp.where(kcol == k, jnp.exp(m - top), val_out)
        logits = jnp.where(col == sel, -jnp.inf, logits)
    idx_ref[...] = idx_out
    gate_ref[...] = val_out / jnp.sum(val_out, axis=1, keepdims=True)


def _outproj_router(yc, ys, ya, h, w_out, g_att, norm_ffn, router_w, router_b, bm=512):
    L, D = h.shape
    full = lambda shape: pl.BlockSpec(shape, lambda i: (0,) * len(shape))
    return pl.pallas_call(
        _outproj_kernel,
        grid=(L // bm,),
        in_specs=[
            pl.BlockSpec((bm, C_CONV), lambda i: (i, 0)),
            pl.BlockSpec((bm, C_SSM), lambda i: (i, 0)),
            pl.BlockSpec((bm, D_ATT), lambda i: (i, 0)),
            pl.BlockSpec((bm, D), lambda i: (i, 0)),
            pl.BlockSpec((C_CONV, D), lambda i: (0, 0)),
            pl.BlockSpec((C_SSM, D), lambda i: (1, 0)),
            pl.BlockSpec((D_ATT, D), lambda i: (1, 0)),
            full((1, D_ATT)),
            full((1, D)),
            full((D, N_EXPERTS)),
            full((1, N_EXPERTS)),
        ],
        out_specs=[
            pl.BlockSpec((bm, D), lambda i: (i, 0)),
            pl.BlockSpec((bm, D), lambda i: (i, 0)),
            pl.BlockSpec((bm, TOP_K), lambda i: (i, 0)),
            pl.BlockSpec((bm, TOP_K), lambda i: (i, 0)),
        ],
        out_shape=[
            jax.ShapeDtypeStruct((L, D), F32),
            jax.ShapeDtypeStruct((L, D), BF16),
            jax.ShapeDtypeStruct((L, TOP_K), jnp.int32),
            jax.ShapeDtypeStruct((L, TOP_K), F32),
        ],
        compiler_params=_cparams(("parallel",)),
        name="outproj_router",
    )(yc, ys, ya, h, w_out, w_out, w_out, g_att.reshape(1, D_ATT),
      norm_ffn.reshape(1, D), router_w, router_b.reshape(1, N_EXPERTS))


GU_GROUP = 2 * LANES


def _gu_perm():
    src = np.arange(GU_GROUP)
    dst = np.where(src % 2 == 0, src // 2, LANES + src // 2)
    p = np.zeros((GU_GROUP, GU_GROUP), np.float32)
    p[src, dst] = 1.0
    return jnp.asarray(p, BF16)


def _wprep_kernel(w_ref, p_ref, o_ref):
    perm = p_ref[...]
    for c in range(w_ref.shape[1] // GU_GROUP):
        blk = w_ref[:, c * GU_GROUP:(c + 1) * GU_GROUP].astype(BF16)
        o_ref[0, :, c * GU_GROUP:(c + 1) * GU_GROUP] = jnp.dot(
            blk, perm, preferred_element_type=F32).astype(BF16)


def _prep_gate_up(w_gu_all, layer, br=1024, bc=1024):
    _, E, D, N = w_gu_all.shape
    return pl.pallas_call(
        _wprep_kernel,
        grid=(E, D // br, N // bc),
        in_specs=[
            pl.BlockSpec((None, None, br, bc), lambda e, i, j: (layer, e, i, j)),
            pl.BlockSpec((GU_GROUP, GU_GROUP), lambda e, i, j: (0, 0)),
        ],
        out_specs=pl.BlockSpec((1, br, bc), lambda e, i, j: (e, i, j)),
        out_shape=jax.ShapeDtypeStruct((E, D, N), BF16),
        compiler_params=_cparams(("parallel", "parallel", "parallel")),
        name="prep_gate_up",
    )(w_gu_all, _gu_perm())


def _moe_kernel(be_ref, nu_ref, x_ref, wgu_ref, bgu_ref, wdn_ref, bdn_ref, g_ref, o_ref, acc_ref):
    b = pl.program_id(0)
    j = pl.program_id(1)
    nj = pl.num_programs(1)

    @pl.when(b < nu_ref[0])
    def _():
        @pl.when(j == 0)
        def _():
            acc_ref[...] = jnp.zeros_like(acc_ref)

        hgu = jnp.dot(x_ref[...], wgu_ref[0], preferred_element_type=F32) + bgu_ref[0]
        parts = []
        for c in range(hgu.shape[1] // GU_GROUP):
            g = jnp.minimum(hgu[:, c * GU_GROUP:c * GU_GROUP + LANES], SWIGLU_LIMIT)
            u = jnp.clip(hgu[:, c * GU_GROUP + LANES:(c + 1) * GU_GROUP],
                         -SWIGLU_LIMIT, SWIGLU_LIMIT)
            parts.append((g * jax.nn.sigmoid(SWIGLU_ALPHA * g) * (u + 1.0)).astype(BF16))
        a = jnp.concatenate(parts, axis=1)
        acc_ref[...] += jnp.dot(a, wdn_ref[...].astype(BF16), preferred_element_type=F32)

        @pl.when(j == nj - 1)
        def _():
            o_ref[...] = ((acc_ref[...] + bdn_ref[0]) * g_ref[...]).astype(o_ref.dtype)


def _moe_experts(xs, block_e, n_used, wgu_p, bgu_p, w_dn_all, layer, b_dn, row_gate, bm, tf=512):
    R, D = xs.shape
    _, E, d_ff, _ = w_dn_all.shape
    nb = R // bm
    nj = d_ff // tf

    def blk(b, nu):
        return jnp.minimum(b, nu[0] - 1)

    grid_spec = pltpu.PrefetchScalarGridSpec(
        num_scalar_prefetch=2,
        grid=(nb, nj),
        in_specs=[
            pl.BlockSpec((bm, D), lambda b, j, be, nu: (blk(b, nu), 0)),
            pl.BlockSpec((1, D, 2 * tf), lambda b, j, be, nu: (be[blk(b, nu)], 0, j)),
            pl.BlockSpec((1, 1, 2 * tf), lambda b, j, be, nu: (be[blk(b, nu)], 0, j)),
            pl.BlockSpec((None, None, tf, D),
                         lambda b, j, be, nu: (layer, be[blk(b, nu)], j, 0)),
            pl.BlockSpec((1, 1, D), lambda b, j, be, nu: (be[blk(b, nu)], 0, 0)),
            pl.BlockSpec((bm, 1), lambda b, j, be, nu: (blk(b, nu), 0)),
        ],
        out_specs=pl.BlockSpec((bm, D), lambda b, j, be, nu: (blk(b, nu), 0)),
        scratch_shapes=[pltpu.VMEM((bm, D), F32)],
    )
    return pl.pallas_call(
        _moe_kernel,
        grid_spec=grid_spec,
        out_shape=jax.ShapeDtypeStruct((R, D), F32),
        compiler_params=_cparams(("arbitrary", "arbitrary")),
        name="moe_experts",
    )(block_e, n_used, xs, wgu_p, bgu_p.reshape(E, 1, 2 * d_ff), w_dn_all,
      b_dn.reshape(E, 1, D), row_gate)


def _moe(h, hn, top_idx, gates, w_gu_all, layer, b_gu, w_dn_all, b_dn, bm=512):
    T, D = h.shape
    E = N_EXPERTS
    n_rows = T * TOP_K
    flat_e = top_idx.reshape(-1)
    flat_g = gates.reshape(-1)
    order = jnp.argsort(flat_e, stable=True).astype(jnp.int32)
    rank = jnp.argsort(order).astype(jnp.int32)
    counts = jnp.sum((flat_e[:, None] == jnp.arange(E)[None, :]).astype(jnp.int32), axis=0)
    padded = (counts + bm - 1) // bm * bm
    start = jnp.cumsum(counts) - counts
    pend = jnp.cumsum(padded)
    pstart = pend - padded
    n_blocks = -(-n_rows // bm) + E
    R = n_blocks * bm
    block_e = jnp.minimum(
        jnp.sum((pend[None, :] <= (jnp.arange(n_blocks) * bm)[:, None]).astype(jnp.int32), axis=1),
        E - 1).astype(jnp.int32)
    n_used = (pend[-1:] // bm).astype(jnp.int32)
    row_e = jnp.repeat(block_e, bm)
    within = jnp.arange(R, dtype=jnp.int32) - pstart[row_e].astype(jnp.int32)
    valid = within < counts[row_e]
    src = order[jnp.clip(start[row_e].astype(jnp.int32) + within, 0, n_rows - 1)]
    buf_tok = jnp.where(valid, src // TOP_K, 0)
    buf_g = jnp.where(valid, flat_g[src], 0.0)
    pos = (rank + (pstart - start).astype(jnp.int32)[flat_e]).reshape(T, TOP_K)

    wgu_p = _prep_gate_up(w_gu_all, layer)
    bgu_p = b_gu.reshape(E, -1, LANES, 2).transpose(0, 1, 3, 2).reshape(E, -1)
    xs = hn[buf_tok]
    ys = _moe_experts(xs, block_e, n_used, wgu_p, bgu_p, w_dn_all, layer, b_dn,
                      buf_g.reshape(R, 1), bm)
    return h + jnp.sum(ys[pos], axis=1)


def _layer(h, p, layer, w_gu_all, w_dn_all):
    w_in = p['w_in']
    scale = LOG2E * HEAD_DIM ** -0.5
    z, f_t = _inproj(
        h, p['norm_mix'].reshape(1, -1),
        w_in[:, :Z_WIDTH].astype(BF16),
        w_in[:, Z_WIDTH:].T.astype(BF16),
        (p['q_norm'] * scale).reshape(1, HEAD_DIM), p['k_norm'].reshape(1, HEAD_DIM))
    c = _cumgate(f_t, p['b_forget'])
    g_conv = p['branch_norm'][:C_CONV]
    g_ssm = p['branch_norm'][C_CONV:C_CONV + C_SSM]
    g_att = p['branch_norm'][C_CONV + C_SSM:]
    yc = _conv_module(z, p['conv_w'], p['conv_b'], p['conv_ln_g'], p['conv_ln_b'], g_conv)
    ys = _s5_module(z, p['ssm_A_re'], p['ssm_A_im'], p['ssm_log_dt'], p['ssm_B_re'],
                    p['ssm_B_im'], p['ssm_C_re'], p['ssm_C_im'], p['ssm_D'],
                    p['ssm_glu_w'], p['ssm_glu_b'], g_ssm)
    ya = _fox_attention(z, c)
    h, hn, top_idx, gates = _outproj_router(
        yc, ys, ya, h, p['w_out'].astype(BF16), g_att, p['norm_ffn'],
        p['router_w'], p['router_b'])
    return _moe(h, hn, top_idx, gates, w_gu_all, layer, p['b_gate_up'], w_dn_all, p['b_down'])


def kernel(x, norm_mix, w_in, b_forget, conv_w, conv_b, conv_ln_g, conv_ln_b, ssm_A_re,
           ssm_A_im, ssm_log_dt, ssm_B_re, ssm_B_im, ssm_C_re, ssm_C_im, ssm_D, ssm_glu_w,
           ssm_glu_b, q_norm, k_norm, branch_norm, w_out, norm_ffn, router_w, router_b,
           w_gate_up, b_gate_up, w_down, b_down):
    params = dict(
        norm_mix=norm_mix, w_in=w_in, b_forget=b_forget, conv_w=conv_w, conv_b=conv_b,
        conv_ln_g=conv_ln_g, conv_ln_b=conv_ln_b, ssm_A_re=ssm_A_re, ssm_A_im=ssm_A_im,
        ssm_log_dt=ssm_log_dt, ssm_B_re=ssm_B_re, ssm_B_im=ssm_B_im, ssm_C_re=ssm_C_re,
        ssm_C_im=ssm_C_im, ssm_D=ssm_D, ssm_glu_w=ssm_glu_w, ssm_glu_b=ssm_glu_b,
        q_norm=q_norm, k_norm=k_norm, branch_norm=branch_norm, w_out=w_out,
        norm_ffn=norm_ffn, router_w=router_w, router_b=router_b,
        b_gate_up=b_gate_up, b_down=b_down)
    bsz, L, D = x.shape
    assert bsz == 1
    h = x.reshape(L, D)
    for l in range(norm_mix.shape[0]):
        h = _layer(h, {k: v[l] for k, v in params.items()}, l, w_gate_up, w_down)
    return h.reshape(bsz, L, D)
```

```python
import functools
import math

import jax
import jax.numpy as jnp
import numpy as np
from jax import lax
from jax.experimental import pallas as pl
from jax.experimental.pallas import tpu as pltpu

F32 = jnp.float32
BF16 = jnp.bfloat16

C_CONV = 512
C_SSM = 512
D_ATT = 1024
HEAD_DIM = 128
N_HEADS = 8
CONV_WIDTH = 31
SSM_GROUP_CH = 16
SSM_GROUPS = 32
SSM_STATE = 64
N_EXPERTS = 32
TOP_K = 4
SWIGLU_LIMIT = 7.0
SWIGLU_ALPHA = 1.702
EPS = 1e-6

LANES = 128
VMEM_LIMIT = 56 * 1024 * 1024

Z_CONV = 0
Z_SSM = 2 * C_CONV
Z_Q = Z_SSM + C_SSM
Z_K = Z_Q + D_ATT
Z_V = Z_K + D_ATT
Z_WIDTH = Z_V + D_ATT
IN_TILE = 512

SSM_CHUNK = 16
SSM_LANE_GROUPS = C_SSM // LANES
SSM_GPL = LANES // SSM_GROUP_CH

NEG_BIG = -1e30


def _cparams(sem, vmem=VMEM_LIMIT):
    return pltpu.CompilerParams(dimension_semantics=sem, vmem_limit_bytes=vmem)


def _inproj_kernel(x_ref, g_ref, w_ref, wf_ref, qg_ref, kg_ref, z_ref, f_ref, hn_ref):
    j = pl.program_id(1)

    @pl.when(j == 0)
    def _():
        x = x_ref[...]
        ms = jnp.mean(x * x, axis=-1, keepdims=True)
        hn = (x * lax.rsqrt(ms + EPS) * g_ref[...]).astype(BF16)
        hn_ref[...] = hn
        f_ref[...] = lax.dot_general(wf_ref[...], hn, (((1,), (1,)), ((), ())),
                                     preferred_element_type=F32)

    acc = jnp.dot(hn_ref[...], w_ref[...], preferred_element_type=F32)
    q_lo, k_lo, v_lo = Z_Q // IN_TILE, Z_K // IN_TILE, Z_V // IN_TILE
    is_q = jnp.logical_and(j >= q_lo, j < k_lo)
    is_k = jnp.logical_and(j >= k_lo, j < v_lo)

    def headnorm(gain):
        for hh in range(IN_TILE // HEAD_DIM):
            a = acc[:, hh * HEAD_DIM:(hh + 1) * HEAD_DIM]
            ms = jnp.mean(a * a, axis=-1, keepdims=True)
            z_ref[:, hh * HEAD_DIM:(hh + 1) * HEAD_DIM] = (
                a * lax.rsqrt(ms + EPS) * gain).astype(z_ref.dtype)

    @pl.when(is_q)
    def _():
        headnorm(qg_ref[...])

    @pl.when(is_k)
    def _():
        headnorm(kg_ref[...])

    @pl.when(jnp.logical_not(jnp.logical_or(is_q, is_k)))
    def _():
        z_ref[...] = acc.astype(z_ref.dtype)


def _inproj(h, g, w_main, w_f, q_gain, k_gain, bm=512):
    L, D = h.shape
    n_tiles = Z_WIDTH // IN_TILE
    return pl.pallas_call(
        _inproj_kernel,
        grid=(L // bm, n_tiles),
        in_specs=[
            pl.BlockSpec((bm, D), lambda i, j: (i, 0)),
            pl.BlockSpec((1, D), lambda i, j: (0, 0)),
            pl.BlockSpec((D, IN_TILE), lambda i, j: (0, j)),
            pl.BlockSpec((N_HEADS, D), lambda i, j: (0, 0)),
            pl.BlockSpec((1, HEAD_DIM), lambda i, j: (0, 0)),
            pl.BlockSpec((1, HEAD_DIM), lambda i, j: (0, 0)),
        ],
        out_specs=[
            pl.BlockSpec((bm, IN_TILE), lambda i, j: (i, j)),
            pl.BlockSpec((N_HEADS, bm), lambda i, j: (0, i)),
        ],
        out_shape=[
            jax.ShapeDtypeStruct((L, Z_WIDTH), BF16),
            jax.ShapeDtypeStruct((N_HEADS, L), F32),
        ],
        scratch_shapes=[pltpu.VMEM((bm, D), BF16)],
        compiler_params=_cparams(("parallel", "arbitrary")),
        name="inproj",
    )(h, g, w_main, w_f, q_gain, k_gain)


def _cumgate_kernel(f_ref, b_ref, c_ref):
    x = jax.nn.log_sigmoid(f_ref[...] + b_ref[...])
    n = x.shape[1]
    pos = lax.broadcasted_iota(jnp.int32, x.shape, 1)
    d = 1
    while d < n:
        x = x + jnp.where(pos >= d, pltpu.roll(x, d, axis=1), 0.0)
        d *= 2
    c_ref[...] = x


def _cumgate(f_t, b_forget):
    H, L = f_t.shape
    return pl.pallas_call(
        _cumgate_kernel,
        out_shape=jax.ShapeDtypeStruct((H, L), F32),
        compiler_params=pltpu.CompilerParams(vmem_limit_bytes=VMEM_LIMIT),
        name="cumgate",
    )(f_t, b_forget.reshape(H, 1))


CONV_HALO = 32


def _conv_kernel(a_ref, gate_ref, cw_ref, cb_ref, lg_ref, lb_ref, bg_ref, o_ref, ubuf):
    i = pl.program_id(0)
    bt = a_ref.shape[0]

    @pl.when(i == 0)
    def _():
        ubuf[0:CONV_HALO, :] = jnp.zeros((CONV_HALO, C_CONV), F32)

    @pl.when(i > 0)
    def _():
        ubuf[0:CONV_HALO, :] = ubuf[bt:bt + CONV_HALO, :]

    a = a_ref[...].astype(F32)
    gate = gate_ref[...].astype(F32)
    ubuf[CONV_HALO:CONV_HALO + bt, :] = a * jax.nn.sigmoid(gate)

    base = CONV_HALO - (CONV_WIDTH - 1)
    acc = jnp.zeros((bt, C_CONV), F32)
    for w in range(CONV_WIDTH):
        acc = acc + ubuf[base + w:base + w + bt, :] * cw_ref[w:w + 1, :]
    y = acc + cb_ref[...]
    mu = jnp.mean(y, axis=-1, keepdims=True)
    yc = y - mu
    var = jnp.mean(yc * yc, axis=-1, keepdims=True)
    y = yc * lax.rsqrt(var + EPS) * lg_ref[...] + lb_ref[...]
    y = y * jax.nn.sigmoid(y)
    ms = jnp.mean(y * y, axis=-1, keepdims=True)
    o_ref[...] = (y * lax.rsqrt(ms + EPS) * bg_ref[...]).astype(o_ref.dtype)


def _conv_module(z, conv_w, conv_b, ln_g, ln_b, g_conv, bt=512):
    L = z.shape[0]
    row = lambda v: v.reshape(1, C_CONV)
    vec = pl.BlockSpec((1, C_CONV), lambda i: (0, 0))
    return pl.pallas_call(
        _conv_kernel,
        grid=(L // bt,),
        in_specs=[
            pl.BlockSpec((bt, C_CONV), lambda i: (i, Z_CONV // C_CONV)),
            pl.BlockSpec((bt, C_CONV), lambda i: (i, Z_CONV // C_CONV + 1)),
            pl.BlockSpec((CONV_WIDTH, C_CONV), lambda i: (0, 0)),
            vec, vec, vec, vec,
        ],
        out_specs=pl.BlockSpec((bt, C_CONV), lambda i: (i, 0)),
        out_shape=jax.ShapeDtypeStruct((L, C_CONV), BF16),
        scratch_shapes=[pltpu.VMEM((bt + CONV_HALO, C_CONV), F32)],
        compiler_params=_cparams(("arbitrary",)),
        name="conv_module",
    )(z, z, conv_w, row(conv_b), row(ln_g), row(ln_b), row(g_conv))


def _ssm_prep_kernel(par_ref, btr_ref, bti_ref, ctr_ref, cti_ref, we_ref, wf_ref, bk_ref, ac_ref):
    T = SSM_CHUNK
    a_re = par_ref[0, 0:1, :]
    a_im = par_ref[0, 1:2, :]
    dt = jnp.exp(par_ref[0, 2:3, :])
    shape = btr_ref.shape[1:]
    same_group = (lax.broadcasted_iota(jnp.int32, shape, 0) // SSM_GROUP_CH
                  == lax.broadcasted_iota(jnp.int32, shape, 1) // SSM_STATE)

    def power(j):
        mag = jnp.exp(float(j) * (dt * a_re))
        ang = float(j) * (dt * a_im)
        return mag * jnp.cos(ang), mag * jnp.sin(ang)

    ab_re, ab_im = power(1)
    inv = 1.0 / (a_re * a_re + a_im * a_im)
    ir, ii = a_re * inv, -a_im * inv
    dr, di = ab_re - 1.0, ab_im
    f_re, f_im = dr * ir - di * ii, dr * ii + di * ir
    btr, bti = btr_ref[0], bti_ref[0]
    bb_re = jnp.where(same_group, f_re * btr - f_im * bti, 0.0)
    bb_im = jnp.where(same_group, f_re * bti + f_im * btr, 0.0)
    c_re = jnp.where(same_group, ctr_ref[0], 0.0)
    c_im = jnp.where(same_group, cti_ref[0], 0.0)
    c0 = jnp.concatenate([c_re, -c_im], axis=1)

    for j in range(T + 1):
        p_re, p_im = power(j)
        if j < T:
            ebar = jnp.concatenate([p_re * bb_re - p_im * bb_im,
                                    p_re * bb_im + p_im * bb_re], axis=1)
            we_ref[0, T - 1 - j] = ebar.astype(we_ref.dtype)
            bk_ref[0, j] = lax.dot_general(
                ebar, c0, (((1,), (1,)), ((), ())), preferred_element_type=F32,
                precision=lax.Precision.HIGHEST).astype(bk_ref.dtype)
        if j >= 1:
            wf_ref[0, j - 1] = jnp.concatenate(
                [c_re * p_re - c_im * p_im, -(c_re * p_im + c_im * p_re)],
                axis=1).astype(wf_ref.dtype)
        if j == T:
            ac_ref[0] = jnp.concatenate([p_re, p_im], axis=1)


def _ssm_prep(A_re, A_im, log_dt, B_re, B_im, C_re, C_im):
    n_t, gpl, H, P, T = SSM_LANE_GROUPS, SSM_GPL, SSM_GROUP_CH, SSM_STATE, SSM_CHUNK
    ns = gpl * P
    par = jnp.stack([A_re.reshape(n_t, ns), A_im.reshape(n_t, ns),
                     jnp.repeat(log_dt, P).reshape(n_t, ns)], axis=1).astype(F32)

    def b_tile(b):
        bt = b.astype(F32).reshape(n_t, gpl, P, H).transpose(0, 1, 3, 2).reshape(n_t, LANES, P)
        return jnp.tile(bt, (1, 1, gpl))

    def c_tile(c):
        return jnp.tile(c.astype(F32).reshape(n_t, LANES, P), (1, 1, gpl))

    tile_spec = pl.BlockSpec((1, LANES, ns), lambda n: (n, 0, 0))
    op_spec = pl.BlockSpec((1, T, LANES, 2 * ns), lambda n: (n, 0, 0, 0))
    w_end, w_fix_t, bker, a_chunk = pl.pallas_call(
        _ssm_prep_kernel,
        grid=(n_t,),
        in_specs=[pl.BlockSpec((1, 3, ns), lambda n: (n, 0, 0)),
                  tile_spec, tile_spec, tile_spec, tile_spec],
        out_specs=[op_spec, op_spec,
                   pl.BlockSpec((1, T, LANES, LANES), lambda n: (n, 0, 0, 0)),
                   pl.BlockSpec((1, 1, 2 * ns), lambda n: (n, 0, 0))],
        out_shape=[jax.ShapeDtypeStruct((n_t, T, LANES, 2 * ns), BF16),
                   jax.ShapeDtypeStruct((n_t, T, LANES, 2 * ns), BF16),
                   jax.ShapeDtypeStruct((n_t, T, LANES, LANES), BF16),
                   jax.ShapeDtypeStruct((n_t, 1, 2 * ns), F32)],
        compiler_params=_cparams(("parallel",)),
        name="ssm_prep",
    )(par, b_tile(B_re), b_tile(B_im), c_tile(C_re), c_tile(C_im))
    return (bker, w_end.reshape(n_t, T * LANES, 2 * ns),
            w_fix_t.reshape(n_t, T * LANES, 2 * ns), a_chunk)


def _ssm_scan_kernel(x_ref, bk_ref, we_ref, wf_ref, a_ref, y_ref, wt_scr, carry, sprev, e_scr):
    r = pl.program_id(1)
    rb = x_ref.shape[1]
    ns = carry.shape[1] // 2
    T = SSM_CHUNK

    @pl.when(r == 0)
    def _():
        carry[...] = jnp.zeros_like(carry)
        for s in range(T):
            for t in range(T):
                blk = bk_ref[0, t - s] if t >= s else jnp.zeros((LANES, LANES), wt_scr.dtype)
                wt_scr[s * LANES:(s + 1) * LANES, t * LANES:(t + 1) * LANES] = blk

    x = x_ref[0]
    y_local = jnp.dot(x, wt_scr[...], preferred_element_type=F32)
    e_scr[...] = jnp.dot(x, we_ref[0], preferred_element_type=F32)
    a_re = a_ref[0, :, 0:ns]
    a_im = a_ref[0, :, ns:2 * ns]

    def step(k, s):
        s_re, s_im = s
        sprev[pl.ds(k, 1), 0:ns] = s_re
        sprev[pl.ds(k, 1), ns:2 * ns] = s_im
        e_re = e_scr[pl.ds(k, 1), 0:ns]
        e_im = e_scr[pl.ds(k, 1), ns:2 * ns]
        return (a_re * s_re - a_im * s_im + e_re, a_re * s_im + a_im * s_re + e_im)

    s_re, s_im = lax.fori_loop(0, rb, step, (carry[:, 0:ns], carry[:, ns:2 * ns]))
    carry[:, 0:ns] = s_re
    carry[:, ns:2 * ns] = s_im
    y_fix = lax.dot_general(sprev[...].astype(BF16), wf_ref[0], (((1,), (1,)), ((), ())),
                            preferred_element_type=F32)
    y_ref[0] = y_local + y_fix


def _ssm_scan(x3, bker, w_end, w_fix_t, a_chunk, rb=256):
    n_t, n_chunks, width = x3.shape
    ns2 = w_end.shape[2]
    rb = min(rb, n_chunks)
    return pl.pallas_call(
        _ssm_scan_kernel,
        grid=(n_t, n_chunks // rb),
        in_specs=[
            pl.BlockSpec((1, rb, width), lambda n, r: (n, r, 0)),
            pl.BlockSpec((1, SSM_CHUNK, LANES, LANES), lambda n, r: (n, 0, 0, 0)),
            pl.BlockSpec((1, width, ns2), lambda n, r: (n, 0, 0)),
            pl.BlockSpec((1, width, ns2), lambda n, r: (n, 0, 0)),
            pl.BlockSpec((1, 1, ns2), lambda n, r: (n, 0, 0)),
        ],
        out_specs=pl.BlockSpec((1, rb, width), lambda n, r: (n, r, 0)),
        out_shape=jax.ShapeDtypeStruct((n_t, n_chunks, width), F32),
        scratch_shapes=[
            pltpu.VMEM((width, width), BF16),
            pltpu.VMEM((1, ns2), F32),
            pltpu.VMEM((rb, ns2), F32),
            pltpu.VMEM((rb, ns2), F32),
        ],
        compiler_params=_cparams(("parallel", "arbitrary")),
        name="ssm_scan",
    )(x3, bker, w_end, w_fix_t, a_chunk)


def _ssm_post_kernel(y_ref, u_ref, d_ref, gw_ref, gb_ref, bg_ref, o_ref):
    y = y_ref[...] + d_ref[...] * u_ref[...].astype(F32)
    y = jax.nn.gelu(y)
    gate = jnp.dot(y.astype(BF16), gw_ref[...], preferred_element_type=F32) + gb_ref[...]
    y = y * jax.nn.sigmoid(gate)
    ms = jnp.mean(y * y, axis=-1, keepdims=True)
    o_ref[...] = (y * lax.rsqrt(ms + EPS) * bg_ref[...]).astype(o_ref.dtype)


def _ssm_post(y_pre, z, ssm_d, glu_w, glu_b, g_ssm, bt=512):
    L = y_pre.shape[0]
    row = lambda v: v.reshape(1, C_SSM)
    vec = pl.BlockSpec((1, C_SSM), lambda i: (0, 0))
    return pl.pallas_call(
        _ssm_post_kernel,
        grid=(L // bt,),
        in_specs=[
            pl.BlockSpec((bt, C_SSM), lambda i: (i, 0)),
            pl.BlockSpec((bt, C_SSM), lambda i: (i, Z_SSM // C_SSM)),
            vec,
            pl.BlockSpec((C_SSM, C_SSM), lambda i: (0, 0)),
            vec, vec,
        ],
        out_specs=pl.BlockSpec((bt, C_SSM), lambda i: (i, 0)),
        out_shape=jax.ShapeDtypeStruct((L, C_SSM), BF16),
        compiler_params=_cparams(("parallel",)),
        name="ssm_post",
    )(y_pre, z, row(ssm_d), glu_w.astype(BF16), row(glu_b), row(g_ssm))


def _s5_module(z, A_re, A_im, log_dt, B_re, B_im, C_re, C_im, ssm_d, glu_w, glu_b, g_ssm):
    L = z.shape[0]
    T = SSM_CHUNK
    ops = _ssm_prep(A_re, A_im, log_dt, B_re, B_im, C_re, C_im)
    u = z[:, Z_SSM:Z_SSM + C_SSM]
    x3 = u.reshape(L // T, T, SSM_LANE_GROUPS, LANES).transpose(2, 0, 1, 3)
    x3 = x3.reshape(SSM_LANE_GROUPS, L // T, T * LANES)
    y3 = _ssm_scan(x3, *ops)
    y_pre = y3.reshape(SSM_LANE_GROUPS, L // T, T, LANES).transpose(1, 2, 0, 3).reshape(L, C_SSM)
    return _ssm_post(y_pre, z, ssm_d, glu_w, glu_b, g_ssm)


BIAS_COLS = 6
LOG2E = 1.4426950408889634


def _attn_kernel(qi_ref, kj_ref, ref_ref, q_ref, aq_ref, k_ref, ak_ref, v_ref, o_ref,
                 qq_scr, m_scr, acc_scr):
    h = pl.program_id(0)
    p = pl.program_id(1)
    i = qi_ref[p]
    j = kj_ref[p]
    bq, bk = q_ref.shape[0], k_ref.shape[0]

    @pl.when(j == 0)
    def _():
        qq_scr[:, 0:HEAD_DIM] = q_ref[...]
        qq_scr[:, HEAD_DIM:2 * HEAD_DIM] = aq_ref[...]
        m_scr[...] = jnp.full_like(m_scr, NEG_BIG)
        acc_scr[...] = jnp.zeros_like(acc_scr)

    lane = lax.broadcasted_iota(jnp.int32, (bk, HEAD_DIM), 1)
    mine = jnp.logical_and(lane >= BIAS_COLS * h, lane < BIAS_COLS * (h + 1))
    ak = jnp.where(mine, ak_ref[...], jnp.zeros_like(ak_ref[...]))
    kk = jnp.concatenate([k_ref[...], ak], axis=1)
    ones_col = jnp.where(lane == 0, 1.0, 0.0).astype(BF16)
    vv = jnp.concatenate([v_ref[...], ones_col], axis=1)
    s = lax.dot_general(qq_scr[...], kk, (((1,), (1,)), ((), ())),
                        preferred_element_type=F32)
    n_blk = ref_ref.shape[0] // N_HEADS
    shift = ref_ref[h * n_blk + i] - ref_ref[h * n_blk + j]

    def update(s):
        m_prev = m_scr[...]
        m_new = jnp.maximum(m_prev, jnp.max(s, axis=1, keepdims=True) + shift)
        alpha = jnp.exp2(m_prev - m_new)
        pr = jnp.exp2(s - (m_new - shift)).astype(BF16)
        acc_scr[...] = alpha * acc_scr[...] + jnp.dot(pr, vv, preferred_element_type=F32)
        m_scr[...] = m_new

    @pl.when(j < i)
    def _():
        update(s)

    @pl.when(j == i)
    def _():
        rows = lax.broadcasted_iota(jnp.int32, (bq, bk), 0)
        cols = lax.broadcasted_iota(jnp.int32, (bq, bk), 1)
        update(jnp.where(cols <= rows, s, NEG_BIG))
        acc = acc_scr[...]
        o_ref[...] = (acc[:, 0:HEAD_DIM] / acc[:, HEAD_DIM:HEAD_DIM + 1]).astype(o_ref.dtype)


def _fox_attention(z, c, blk=2048):
    L = z.shape[0]
    blk = min(blk, L)
    nb = L // blk
    qi = np.repeat(np.arange(nb), np.arange(1, nb + 1)).astype(np.int32)
    kj = np.concatenate([np.arange(i + 1) for i in range(nb)]).astype(np.int32)

    c2 = (c * LOG2E).reshape(N_HEADS, nb, blk)
    ref = c2[:, :, 0]
    d = (c2 - ref[:, :, None]).reshape(N_HEADS, L).T

    def bf16_part(v):
        bits = lax.bitcast_convert_type(v, jnp.uint32) & jnp.uint32(0xFFFF0000)
        return lax.bitcast_convert_type(bits, F32)

    p1 = bf16_part(d)
    p2 = bf16_part(d - p1)
    p3 = bf16_part(d - p1 - p2)
    d1, d2, d3 = p1.astype(BF16), p2.astype(BF16), p3.astype(BF16)
    one = jnp.ones_like(d1)
    pad = jnp.zeros((L, HEAD_DIM - N_HEADS * BIAS_COLS), BF16)
    aq = jnp.concatenate(
        [jnp.stack([d1, d2, d3, one, one, one], axis=-1).reshape(L, -1), pad], axis=1)
    ak = jnp.concatenate(
        [jnp.stack([one, one, one, -d1, -d2, -d3], axis=-1).reshape(L, -1), pad], axis=1)

    q0, k0, v0 = Z_Q // HEAD_DIM, Z_K // HEAD_DIM, Z_V // HEAD_DIM
    grid_spec = pltpu.PrefetchScalarGridSpec(
        num_scalar_prefetch=3,
        grid=(N_HEADS, len(qi)),
        in_specs=[
            pl.BlockSpec((blk, HEAD_DIM), lambda h, p, qi, kj, rf: (qi[p], q0 + h)),
            pl.BlockSpec((blk, HEAD_DIM), lambda h, p, qi, kj, rf: (qi[p], 0)),
            pl.BlockSpec((blk, HEAD_DIM), lambda h, p, qi, kj, rf: (kj[p], k0 + h)),
            pl.BlockSpec((blk, HEAD_DIM), lambda h, p, qi, kj, rf: (kj[p], 0)),
            pl.BlockSpec((blk, HEAD_DIM), lambda h, p, qi, kj, rf: (kj[p], v0 + h)),
        ],
        out_specs=pl.BlockSpec((blk, HEAD_DIM), lambda h, p, qi, kj, rf: (qi[p], h)),
        scratch_shapes=[
            pltpu.VMEM((blk, 2 * HEAD_DIM), BF16),
            pltpu.VMEM((blk, 1), F32),
            pltpu.VMEM((blk, 2 * HEAD_DIM), F32),
        ],
    )
    return pl.pallas_call(
        _attn_kernel,
        grid_spec=grid_spec,
        out_shape=jax.ShapeDtypeStruct((L, D_ATT), BF16),
        compiler_params=_cparams(("parallel", "arbitrary")),
        name="fox_attention",
    )(jnp.asarray(qi), jnp.asarray(kj), ref.reshape(-1), z, aq, z, ak, z)


def _outproj_kernel(yc_ref, ys_ref, ya_ref, h_ref, wc_ref, ws_ref, wa_ref, ga_ref,
                    nf_ref, rw_ref, rb_ref, ho_ref, hn_ref, idx_ref, gate_ref):
    ya = ya_ref[...].astype(F32)
    ms = jnp.mean(ya * ya, axis=-1, keepdims=True)
    ya = (ya * lax.rsqrt(ms + EPS) * ga_ref[...]).astype(BF16)
    mix = jnp.dot(yc_ref[...], wc_ref[...], preferred_element_type=F32)
    mix = mix + jnp.dot(ys_ref[...], ws_ref[...], preferred_element_type=F32)
    mix = mix + jnp.dot(ya, wa_ref[...], preferred_element_type=F32)
    h = h_ref[...] + mix
    ho_ref[...] = h

    ms = jnp.mean(h * h, axis=-1, keepdims=True)
    hn = h * lax.rsqrt(ms + EPS) * nf_ref[...]
    hn_ref[...] = hn.astype(hn_ref.dtype)
    logits = jnp.dot(hn, rw_ref[...], preferred_element_type=F32,
                     precision=lax.Precision.HIGHEST) + rb_ref[...]
    col = lax.broadcasted_iota(jnp.int32, logits.shape, 1)
    kcol = lax.broadcasted_iota(jnp.int32, idx_ref.shape, 1)
    idx_out = jnp.zeros(idx_ref.shape, jnp.int32)
    val_out = jnp.zeros(gate_ref.shape, F32)
    top = None
    for k in range(TOP_K):
        m = jnp.max(logits, axis=1, keepdims=True)
        sel = jnp.min(jnp.where(logits == m, col, N_EXPERTS), axis=1, keepdims=True)
        if top is None:
            top = m
        idx_out = jnp.where(kcol == k, sel, idx_out)
        val_out = jnp.where(kcol == k, jnp.exp(m - top), val_out)
        logits = jnp.where(col == sel, -jnp.inf, logits)
    idx_ref[...] = idx_out
    gate_ref[...] = val_out / jnp.sum(val_out, axis=1, keepdims=True)


def _outproj_router(yc, ys, ya, h, w_out, g_att, norm_ffn, router_w, router_b, bm=512):
    L, D = h.shape
    full = lambda shape: pl.BlockSpec(shape, lambda i: (0,) * len(shape))
    return pl.pallas_call(
        _outproj_kernel,
        grid=(L // bm,),
        in_specs=[
            pl.BlockSpec((bm, C_CONV), lambda i: (i, 0)),
            pl.BlockSpec((bm, C_SSM), lambda i: (i, 0)),
            pl.BlockSpec((bm, D_ATT), lambda i: (i, 0)),
            pl.BlockSpec((bm, D), lambda i: (i, 0)),
            pl.BlockSpec((C_CONV, D), lambda i: (0, 0)),
            pl.BlockSpec((C_SSM, D), lambda i: (1, 0)),
            pl.BlockSpec((D_ATT, D), lambda i: (1, 0)),
            full((1, D_ATT)),
            full((1, D)),
            full((D, N_EXPERTS)),
            full((1, N_EXPERTS)),
        ],
        out_specs=[
            pl.BlockSpec((bm, D), lambda i: (i, 0)),
            pl.BlockSpec((bm, D), lambda i: (i, 0)),
            pl.BlockSpec((bm, TOP_K), lambda i: (i, 0)),
            pl.BlockSpec((bm, TOP_K), lambda i: (i, 0)),
        ],
        out_shape=[
            jax.ShapeDtypeStruct((L, D), F32),
            jax.ShapeDtypeStruct((L, D), BF16),
            jax.ShapeDtypeStruct((L, TOP_K), jnp.int32),
            jax.ShapeDtypeStruct((L, TOP_K), F32),
        ],
        compiler_params=_cparams(("parallel",)),
        name="outproj_router",
    )(yc, ys, ya, h, w_out, w_out, w_out, g_att.reshape(1, D_ATT),
      norm_ffn.reshape(1, D), router_w, router_b.reshape(1, N_EXPERTS))


GU_GROUP = 2 * LANES
MOE_CHUNKS = 2


def _gu_perm():
    src = np.arange(GU_GROUP)
    dst = np.where(src % 2 == 0, src // 2, LANES + src // 2)
    p = np.zeros((GU_GROUP, GU_GROUP), np.float32)
    p[src, dst] = 1.0
    return jnp.asarray(p, BF16)


def _wprep_kernel(w_ref, p_ref, o_ref):
    perm = p_ref[...]
    for c in range(w_ref.shape[1] // GU_GROUP):
        blk = w_ref[:, c * GU_GROUP:(c + 1) * GU_GROUP].astype(BF16)
        o_ref[0, :, c * GU_GROUP:(c + 1) * GU_GROUP] = jnp.dot(
            blk, perm, preferred_element_type=F32).astype(BF16)


def _prep_gate_up(w_gu_all, layer, br=1024, bc=1024):
    _, E, D, N = w_gu_all.shape
    return pl.pallas_call(
        _wprep_kernel,
        grid=(E, D // br, N // bc),
        in_specs=[
            pl.BlockSpec((None, None, br, bc), lambda e, i, j: (layer, e, i, j)),
            pl.BlockSpec((GU_GROUP, GU_GROUP), lambda e, i, j: (0, 0)),
        ],
        out_specs=pl.BlockSpec((1, br, bc), lambda e, i, j: (e, i, j)),
        out_shape=jax.ShapeDtypeStruct((E, D, N), BF16),
        compiler_params=_cparams(("parallel", "parallel", "parallel")),
        name="prep_gate_up",
    )(w_gu_all, _gu_perm())


def _moe_kernel(be_ref, nu_ref, x_ref, wgu_ref, bgu_ref, wdn_ref, bdn_ref, g_ref, *rest):
    o_ref, acc_ref = rest[-2:]
    b = pl.program_id(0)
    j = pl.program_id(1)
    nj = pl.num_programs(1)
    used = b < nu_ref[0]

    @pl.when(jnp.logical_and(jnp.logical_not(used), j == nj - 1))
    def _():
        o_ref[...] = jnp.zeros_like(o_ref)

    @pl.when(used)
    def _():
        @pl.when(j == 0)
        def _():
            acc_ref[...] = jnp.zeros_like(acc_ref)

        hgu = jnp.dot(x_ref[...], wgu_ref[0], preferred_element_type=F32) + bgu_ref[0]
        parts = []
        for c in range(hgu.shape[1] // GU_GROUP):
            g = jnp.minimum(hgu[:, c * GU_GROUP:c * GU_GROUP + LANES], SWIGLU_LIMIT)
            u = jnp.clip(hgu[:, c * GU_GROUP + LANES:(c + 1) * GU_GROUP],
                         -SWIGLU_LIMIT, SWIGLU_LIMIT)
            parts.append((g * jax.nn.sigmoid(SWIGLU_ALPHA * g) * (u + 1.0)).astype(BF16))
        a = jnp.concatenate(parts, axis=1)
        acc_ref[...] += jnp.dot(a, wdn_ref[...].astype(BF16), preferred_element_type=F32)

        @pl.when(j == nj - 1)
        def _():
            o_ref[...] = ((acc_ref[...] + bdn_ref[0]) * g_ref[...]).astype(o_ref.dtype)


def _moe_experts(xs, block_e, n_used, wgu_p, bgu_p, w_dn_all, layer, b_dn, row_gate, bm, tf=512,
                 first_block=0, total_rows=None, ys_prev=None):
    rows, D = xs.shape
    R = total_rows or rows
    _, E, d_ff, _ = w_dn_all.shape
    nb = rows // bm
    nj = d_ff // tf

    def blk(b, nu):
        return jnp.maximum(jnp.minimum(b, nu[0] - 1), 0)

    args = [block_e, n_used, xs, wgu_p, bgu_p.reshape(E, 1, 2 * d_ff), w_dn_all,
            b_dn.reshape(E, 1, D), row_gate]
    extra_specs, aliases = [], {}
    if ys_prev is not None:
        extra_specs = [pl.BlockSpec(memory_space=pl.ANY)]
        aliases = {len(args): 0}
        args.append(ys_prev)

    grid_spec = pltpu.PrefetchScalarGridSpec(
        num_scalar_prefetch=2,
        grid=(nb, nj),
        in_specs=[
            pl.BlockSpec((bm, D), lambda b, j, be, nu: (blk(b, nu), 0)),
            pl.BlockSpec((1, D, 2 * tf), lambda b, j, be, nu: (be[blk(b, nu)], 0, j)),
            pl.BlockSpec((1, 1, 2 * tf), lambda b, j, be, nu: (be[blk(b, nu)], 0, j)),
            pl.BlockSpec((None, None, tf, D),
                         lambda b, j, be, nu: (layer, be[blk(b, nu)], j, 0)),
            pl.BlockSpec((1, 1, D), lambda b, j, be, nu: (be[blk(b, nu)], 0, 0)),
            pl.BlockSpec((bm, 1), lambda b, j, be, nu: (blk(b, nu), 0)),
        ] + extra_specs,
        out_specs=pl.BlockSpec((bm, D), lambda b, j, be, nu: (b + first_block, 0)),
        scratch_shapes=[pltpu.VMEM((bm, D), F32)],
    )
    return pl.pallas_call(
        _moe_kernel,
        grid_spec=grid_spec,
        out_shape=jax.ShapeDtypeStruct((R, D), F32),
        input_output_aliases=aliases,
        compiler_params=_cparams(("arbitrary", "arbitrary")),
        name="moe_experts",
    )(*args)


def _moe(h, hn, top_idx, gates, w_gu_all, layer, b_gu, w_dn_all, b_dn, bm=512):
    T, D = h.shape
    E = N_EXPERTS
    n_rows = T * TOP_K
    flat_e = top_idx.reshape(-1)
    flat_g = gates.reshape(-1)
    order = jnp.argsort(flat_e, stable=True).astype(jnp.int32)
    rank = jnp.argsort(order).astype(jnp.int32)
    counts = jnp.sum((flat_e[None, :] == jnp.arange(E)[:, None]).astype(jnp.int32), axis=1)
    padded = (counts + bm - 1) // bm * bm
    start = jnp.cumsum(counts) - counts
    pend = jnp.cumsum(padded)
    pstart = pend - padded
    n_blocks = -(-n_rows // bm) + E
    R = n_blocks * bm
    block_e = jnp.minimum(
        jnp.sum((pend[None, :] <= (jnp.arange(n_blocks) * bm)[:, None]).astype(jnp.int32), axis=1),
        E - 1).astype(jnp.int32)
    n_used = (pend[-1:] // bm).astype(jnp.int32)
    row_e = jnp.repeat(block_e, bm)
    within = jnp.arange(R, dtype=jnp.int32) - pstart[row_e].astype(jnp.int32)
    valid = within < counts[row_e]
    src = order[jnp.clip(start[row_e].astype(jnp.int32) + within, 0, n_rows - 1)]
    buf_tok = jnp.where(valid, src // TOP_K, 0)
    buf_g = jnp.where(valid, flat_g[src], 0.0)
    pos = (rank + (pstart - start).astype(jnp.int32)[flat_e]).reshape(T, TOP_K)

    wgu_p = _prep_gate_up(w_gu_all, layer)
    bgu_p = b_gu.reshape(E, -1, LANES, 2).transpose(0, 1, 3, 2).reshape(E, -1)
    cb = n_blocks // MOE_CHUNKS
    ys = None
    for c in range(MOE_CHUNKS):
        rows = slice(c * cb * bm, (c + 1) * cb * bm)
        ys = _moe_experts(
            hn[buf_tok[rows]], block_e[c * cb:(c + 1) * cb], jnp.clip(n_used - c * cb, 0, cb),
            wgu_p, bgu_p, w_dn_all, layer, b_dn, buf_g[rows].reshape(-1, 1), bm,
            first_block=c * cb, total_rows=R, ys_prev=ys)
    return _moe_combine(h, ys[pos.T])


def _combine_kernel(h_ref, y_ref, o_ref):
    acc = h_ref[...]
    for k in range(TOP_K):
        acc = acc + y_ref[k]
    o_ref[...] = acc


def _moe_combine(h, y4, tb=256):
    T, D = h.shape
    return pl.pallas_call(
        _combine_kernel,
        grid=(T // tb,),
        in_specs=[
            pl.BlockSpec((tb, D), lambda i: (i, 0)),
            pl.BlockSpec((TOP_K, tb, D), lambda i: (0, i, 0)),
        ],
        out_specs=pl.BlockSpec((tb, D), lambda i: (i, 0)),
        out_shape=jax.ShapeDtypeStruct((T, D), F32),
        compiler_params=_cparams(("parallel",)),
        name="moe_combine",
    )(h, y4)


def _layer(h, p, layer, w_gu_all, w_dn_all):
    w_in = p['w_in']
    scale = LOG2E * HEAD_DIM ** -0.5
    z, f_t = _inproj(
        h, p['norm_mix'].reshape(1, -1),
        w_in[:, :Z_WIDTH].astype(BF16),
        w_in[:, Z_WIDTH:].T.astype(BF16),
        (p['q_norm'] * scale).reshape(1, HEAD_DIM), p['k_norm'].reshape(1, HEAD_DIM))
    c = _cumgate(f_t, p['b_forget'])
    g_conv = p['branch_norm'][:C_CONV]
    g_ssm = p['branch_norm'][C_CONV:C_CONV + C_SSM]
    g_att = p['branch_norm'][C_CONV + C_SSM:]
    yc = _conv_module(z, p['conv_w'], p['conv_b'], p['conv_ln_g'], p['conv_ln_b'], g_conv)
    ys = _s5_module(z, p['ssm_A_re'], p['ssm_A_im'], p['ssm_log_dt'], p['ssm_B_re'],
                    p['ssm_B_im'], p['ssm_C_re'], p['ssm_C_im'], p['ssm_D'],
                    p['ssm_glu_w'], p['ssm_glu_b'], g_ssm)
    ya = _fox_attention(z, c)
    h, hn, top_idx, gates = _outproj_router(
        yc, ys, ya, h, p['w_out'].astype(BF16), g_att, p['norm_ffn'],
        p['router_w'], p['router_b'])
    return _moe(h, hn, top_idx, gates, w_gu_all, layer, p['b_gate_up'], w_dn_all, p['b_down'])


def kernel(x, norm_mix, w_in, b_forget, conv_w, conv_b, conv_ln_g, conv_ln_b, ssm_A_re,
           ssm_A_im, ssm_log_dt, ssm_B_re, ssm_B_im, ssm_C_re, ssm_C_im, ssm_D, ssm_glu_w,
           ssm_glu_b, q_norm, k_norm, branch_norm, w_out, norm_ffn, router_w, router_b,
           w_gate_up, b_gate_up, w_down, b_down):
    params = dict(
        norm_mix=norm_mix, w_in=w_in, b_forget=b_forget, conv_w=conv_w, conv_b=conv_b,
        conv_ln_g=conv_ln_g, conv_ln_b=conv_ln_b, ssm_A_re=ssm_A_re, ssm_A_im=ssm_A_im,
        ssm_log_dt=ssm_log_dt, ssm_B_re=ssm_B_re, ssm_B_im=ssm_B_im, ssm_C_re=ssm_C_re,
        ssm_C_im=ssm_C_im, ssm_D=ssm_D, ssm_glu_w=ssm_glu_w, ssm_glu_b=ssm_glu_b,
        q_norm=q_norm, k_norm=k_norm, branch_norm=branch_norm, w_out=w_out,
        norm_ffn=norm_ffn, router_w=router_w, router_b=router_b,
        b_gate_up=b_gate_up, b_down=b_down)
    bsz, L, D = x.shape
    assert bsz == 1
    h = x.reshape(L, D)
    for l in range(norm_mix.shape[0]):
        h = _layer(h, {k: v[l] for k, v in params.items()}, l, w_gate_up, w_down)
    return h.reshape(bsz, L, D)
```

```python
import functools
import math

import jax
import jax.numpy as jnp
import numpy as np
from jax import lax
from jax.experimental import pallas as pl
from jax.experimental.pallas import tpu as pltpu

F32 = jnp.float32
BF16 = jnp.bfloat16

C_CONV = 512
C_SSM = 512
D_ATT = 1024
HEAD_DIM = 128
N_HEADS = 8
CONV_WIDTH = 31
SSM_GROUP_CH = 16
SSM_GROUPS = 32
SSM_STATE = 64
N_EXPERTS = 32
TOP_K = 4
SWIGLU_LIMIT = 7.0
SWIGLU_ALPHA = 1.702
EPS = 1e-6

LANES = 128
VMEM_LIMIT = 56 * 1024 * 1024

Z_CONV = 0
Z_SSM = 2 * C_CONV
Z_Q = Z_SSM + C_SSM
Z_K = Z_Q + D_ATT
Z_V = Z_K + D_ATT
Z_WIDTH = Z_V + D_ATT
IN_TILE = 512

SSM_CHUNK = 16
SSM_LANE_GROUPS = C_SSM // LANES
SSM_GPL = LANES // SSM_GROUP_CH

NEG_BIG = -1e30


def _cparams(sem, vmem=VMEM_LIMIT):
    return pltpu.CompilerParams(dimension_semantics=sem, vmem_limit_bytes=vmem)


def _inproj_kernel(x_ref, g_ref, w_ref, wf_ref, qg_ref, kg_ref, z_ref, f_ref, hn_ref):
    j = pl.program_id(1)

    @pl.when(j == 0)
    def _():
        x = x_ref[...]
        ms = jnp.mean(x * x, axis=-1, keepdims=True)
        hn = (x * lax.rsqrt(ms + EPS) * g_ref[...]).astype(BF16)
        hn_ref[...] = hn
        f_ref[...] = lax.dot_general(wf_ref[...], hn, (((1,), (1,)), ((), ())),
                                     preferred_element_type=F32)

    acc = jnp.dot(hn_ref[...], w_ref[...], preferred_element_type=F32)
    q_lo, k_lo, v_lo = Z_Q // IN_TILE, Z_K // IN_TILE, Z_V // IN_TILE
    is_q = jnp.logical_and(j >= q_lo, j < k_lo)
    is_k = jnp.logical_and(j >= k_lo, j < v_lo)

    def headnorm(gain):
        for hh in range(IN_TILE // HEAD_DIM):
            a = acc[:, hh * HEAD_DIM:(hh + 1) * HEAD_DIM]
            ms = jnp.mean(a * a, axis=-1, keepdims=True)
            z_ref[:, hh * HEAD_DIM:(hh + 1) * HEAD_DIM] = (
                a * lax.rsqrt(ms + EPS) * gain).astype(z_ref.dtype)

    @pl.when(is_q)
    def _():
        headnorm(qg_ref[...])

    @pl.when(is_k)
    def _():
        headnorm(kg_ref[...])

    @pl.when(jnp.logical_not(jnp.logical_or(is_q, is_k)))
    def _():
        z_ref[...] = acc.astype(z_ref.dtype)


def _inproj(h, g, w_main, w_f, q_gain, k_gain, bm=1024):
    L, D = h.shape
    n_tiles = Z_WIDTH // IN_TILE
    return pl.pallas_call(
        _inproj_kernel,
        grid=(L // bm, n_tiles),
        in_specs=[
            pl.BlockSpec((bm, D), lambda i, j: (i, 0)),
            pl.BlockSpec((1, D), lambda i, j: (0, 0)),
            pl.BlockSpec((D, IN_TILE), lambda i, j: (0, j)),
            pl.BlockSpec((N_HEADS, D), lambda i, j: (0, 0)),
            pl.BlockSpec((1, HEAD_DIM), lambda i, j: (0, 0)),
            pl.BlockSpec((1, HEAD_DIM), lambda i, j: (0, 0)),
        ],
        out_specs=[
            pl.BlockSpec((bm, IN_TILE), lambda i, j: (i, j)),
            pl.BlockSpec((N_HEADS, bm), lambda i, j: (0, i)),
        ],
        out_shape=[
            jax.ShapeDtypeStruct((L, Z_WIDTH), BF16),
            jax.ShapeDtypeStruct((N_HEADS, L), F32),
        ],
        scratch_shapes=[pltpu.VMEM((bm, D), BF16)],
        compiler_params=_cparams(("parallel", "arbitrary")),
        name="inproj",
    )(h, g, w_main, w_f, q_gain, k_gain)


def _cumgate_kernel(f_ref, b_ref, c_ref):
    x = jax.nn.log_sigmoid(f_ref[...] + b_ref[...])
    n = x.shape[1]
    pos = lax.broadcasted_iota(jnp.int32, x.shape, 1)
    d = 1
    while d < n:
        x = x + jnp.where(pos >= d, pltpu.roll(x, d, axis=1), 0.0)
        d *= 2
    c_ref[...] = x


def _cumgate(f_t, b_forget):
    H, L = f_t.shape
    return pl.pallas_call(
        _cumgate_kernel,
        out_shape=jax.ShapeDtypeStruct((H, L), F32),
        compiler_params=pltpu.CompilerParams(vmem_limit_bytes=VMEM_LIMIT),
        name="cumgate",
    )(f_t, b_forget.reshape(H, 1))


CONV_HALO = 32


def _conv_kernel(a_ref, gate_ref, cw_ref, cb_ref, lg_ref, lb_ref, bg_ref, o_ref, ubuf):
    i = pl.program_id(0)
    bt = a_ref.shape[0]

    @pl.when(i == 0)
    def _():
        ubuf[0:CONV_HALO, :] = jnp.zeros((CONV_HALO, C_CONV), F32)

    @pl.when(i > 0)
    def _():
        ubuf[0:CONV_HALO, :] = ubuf[bt:bt + CONV_HALO, :]

    a = a_ref[...].astype(F32)
    gate = gate_ref[...].astype(F32)
    ubuf[CONV_HALO:CONV_HALO + bt, :] = a * jax.nn.sigmoid(gate)

    base = CONV_HALO - (CONV_WIDTH - 1)
    acc = jnp.zeros((bt, C_CONV), F32)
    for w in range(CONV_WIDTH):
        acc = acc + ubuf[base + w:base + w + bt, :] * cw_ref[w:w + 1, :]
    y = acc + cb_ref[...]
    mu = jnp.mean(y, axis=-1, keepdims=True)
    yc = y - mu
    var = jnp.mean(yc * yc, axis=-1, keepdims=True)
    y = yc * lax.rsqrt(var + EPS) * lg_ref[...] + lb_ref[...]
    y = y * jax.nn.sigmoid(y)
    ms = jnp.mean(y * y, axis=-1, keepdims=True)
    o_ref[...] = (y * lax.rsqrt(ms + EPS) * bg_ref[...]).astype(o_ref.dtype)


def _conv_module(z, conv_w, conv_b, ln_g, ln_b, g_conv, bt=512):
    L = z.shape[0]
    row = lambda v: v.reshape(1, C_CONV)
    vec = pl.BlockSpec((1, C_CONV), lambda i: (0, 0))
    return pl.pallas_call(
        _conv_kernel,
        grid=(L // bt,),
        in_specs=[
            pl.BlockSpec((bt, C_CONV), lambda i: (i, Z_CONV // C_CONV)),
            pl.BlockSpec((bt, C_CONV), lambda i: (i, Z_CONV // C_CONV + 1)),
            pl.BlockSpec((CONV_WIDTH, C_CONV), lambda i: (0, 0)),
            vec, vec, vec, vec,
        ],
        out_specs=pl.BlockSpec((bt, C_CONV), lambda i: (i, 0)),
        out_shape=jax.ShapeDtypeStruct((L, C_CONV), BF16),
        scratch_shapes=[pltpu.VMEM((bt + CONV_HALO, C_CONV), F32)],
        compiler_params=_cparams(("arbitrary",)),
        name="conv_module",
    )(z, z, conv_w, row(conv_b), row(ln_g), row(ln_b), row(g_conv))


def _ssm_prep_kernel(par_ref, btr_ref, bti_ref, ctr_ref, cti_ref, we_ref, wf_ref, bk_ref, ac_ref):
    T = SSM_CHUNK
    a_re = par_ref[0, 0:1, :]
    a_im = par_ref[0, 1:2, :]
    dt = jnp.exp(par_ref[0, 2:3, :])
    shape = btr_ref.shape[1:]
    same_group = (lax.broadcasted_iota(jnp.int32, shape, 0) // SSM_GROUP_CH
                  == lax.broadcasted_iota(jnp.int32, shape, 1) // SSM_STATE)

    def power(j):
        mag = jnp.exp(float(j) * (dt * a_re))
        ang = float(j) * (dt * a_im)
        return mag * jnp.cos(ang), mag * jnp.sin(ang)

    ab_re, ab_im = power(1)
    inv = 1.0 / (a_re * a_re + a_im * a_im)
    ir, ii = a_re * inv, -a_im * inv
    dr, di = ab_re - 1.0, ab_im
    f_re, f_im = dr * ir - di * ii, dr * ii + di * ir
    btr, bti = btr_ref[0], bti_ref[0]
    bb_re = jnp.where(same_group, f_re * btr - f_im * bti, 0.0)
    bb_im = jnp.where(same_group, f_re * bti + f_im * btr, 0.0)
    c_re = jnp.where(same_group, ctr_ref[0], 0.0)
    c_im = jnp.where(same_group, cti_ref[0], 0.0)
    c0 = jnp.concatenate([c_re, -c_im], axis=1)

    for j in range(T + 1):
        p_re, p_im = power(j)
        if j < T:
            ebar = jnp.concatenate([p_re * bb_re - p_im * bb_im,
                                    p_re * bb_im + p_im * bb_re], axis=1)
            we_ref[0, T - 1 - j] = ebar.astype(we_ref.dtype)
            bk_ref[0, j] = lax.dot_general(
                ebar, c0, (((1,), (1,)), ((), ())), preferred_element_type=F32,
                precision=lax.Precision.HIGHEST).astype(bk_ref.dtype)
        if j >= 1:
            wf_ref[0, j - 1] = jnp.concatenate(
                [c_re * p_re - c_im * p_im, -(c_re * p_im + c_im * p_re)],
                axis=1).astype(wf_ref.dtype)
        if j == T:
            ac_ref[0] = jnp.concatenate([p_re, p_im], axis=1)


def _ssm_prep(A_re, A_im, log_dt, B_re, B_im, C_re, C_im):
    n_t, gpl, H, P, T = SSM_LANE_GROUPS, SSM_GPL, SSM_GROUP_CH, SSM_STATE, SSM_CHUNK
    ns = gpl * P
    par = jnp.stack([A_re.reshape(n_t, ns), A_im.reshape(n_t, ns),
                     jnp.repeat(log_dt, P).reshape(n_t, ns)], axis=1).astype(F32)

    def b_tile(b):
        bt = b.astype(F32).reshape(n_t, gpl, P, H).transpose(0, 1, 3, 2).reshape(n_t, LANES, P)
        return jnp.tile(bt, (1, 1, gpl))

    def c_tile(c):
        return jnp.tile(c.astype(F32).reshape(n_t, LANES, P), (1, 1, gpl))

    tile_spec = pl.BlockSpec((1, LANES, ns), lambda n: (n, 0, 0))
    op_spec = pl.BlockSpec((1, T, LANES, 2 * ns), lambda n: (n, 0, 0, 0))
    w_end, w_fix_t, bker, a_chunk = pl.pallas_call(
        _ssm_prep_kernel,
        grid=(n_t,),
        in_specs=[pl.BlockSpec((1, 3, ns), lambda n: (n, 0, 0)),
                  tile_spec, tile_spec, tile_spec, tile_spec],
        out_specs=[op_spec, op_spec,
                   pl.BlockSpec((1, T, LANES, LANES), lambda n: (n, 0, 0, 0)),
                   pl.BlockSpec((1, 1, 2 * ns), lambda n: (n, 0, 0))],
        out_shape=[jax.ShapeDtypeStruct((n_t, T, LANES, 2 * ns), BF16),
                   jax.ShapeDtypeStruct((n_t, T, LANES, 2 * ns), BF16),
                   jax.ShapeDtypeStruct((n_t, T, LANES, LANES), BF16),
                   jax.ShapeDtypeStruct((n_t, 1, 2 * ns), F32)],
        compiler_params=_cparams(("parallel",)),
        name="ssm_prep",
    )(par, b_tile(B_re), b_tile(B_im), c_tile(C_re), c_tile(C_im))
    return (bker, w_end.reshape(n_t, T * LANES, 2 * ns),
            w_fix_t.reshape(n_t, T * LANES, 2 * ns), a_chunk)


def _ssm_scan_kernel(x_ref, bk_ref, we_ref, wf_ref, a_ref, y_ref, wt_scr, carry, sprev, e_scr):
    r = pl.program_id(1)
    rb = x_ref.shape[1]
    ns = carry.shape[1] // 2
    T = SSM_CHUNK

    @pl.when(r == 0)
    def _():
        carry[...] = jnp.zeros_like(carry)
        for s in range(T):
            for t in range(T):
                blk = bk_ref[0, t - s] if t >= s else jnp.zeros((LANES, LANES), wt_scr.dtype)
                wt_scr[s * LANES:(s + 1) * LANES, t * LANES:(t + 1) * LANES] = blk

    x = x_ref[0]
    y_local = jnp.dot(x, wt_scr[...], preferred_element_type=F32)
    e_scr[...] = jnp.dot(x, we_ref[0], preferred_element_type=F32)
    a_re = a_ref[0, :, 0:ns]
    a_im = a_ref[0, :, ns:2 * ns]

    def step(k, s):
        s_re, s_im = s
        sprev[pl.ds(k, 1), 0:ns] = s_re
        sprev[pl.ds(k, 1), ns:2 * ns] = s_im
        e_re = e_scr[pl.ds(k, 1), 0:ns]
        e_im = e_scr[pl.ds(k, 1), ns:2 * ns]
        return (a_re * s_re - a_im * s_im + e_re, a_re * s_im + a_im * s_re + e_im)

    s_re, s_im = lax.fori_loop(0, rb, step, (carry[:, 0:ns], carry[:, ns:2 * ns]))
    carry[:, 0:ns] = s_re
    carry[:, ns:2 * ns] = s_im
    y_fix = lax.dot_general(sprev[...].astype(BF16), wf_ref[0], (((1,), (1,)), ((), ())),
                            preferred_element_type=F32)
    y_ref[0] = y_local + y_fix


def _ssm_scan(x3, bker, w_end, w_fix_t, a_chunk, rb=256):
    n_t, n_chunks, width = x3.shape
    ns2 = w_end.shape[2]
    rb = min(rb, n_chunks)
    return pl.pallas_call(
        _ssm_scan_kernel,
        grid=(n_t, n_chunks // rb),
        in_specs=[
            pl.BlockSpec((1, rb, width), lambda n, r: (n, r, 0)),
            pl.BlockSpec((1, SSM_CHUNK, LANES, LANES), lambda n, r: (n, 0, 0, 0)),
            pl.BlockSpec((1, width, ns2), lambda n, r: (n, 0, 0)),
            pl.BlockSpec((1, width, ns2), lambda n, r: (n, 0, 0)),
            pl.BlockSpec((1, 1, ns2), lambda n, r: (n, 0, 0)),
        ],
        out_specs=pl.BlockSpec((1, rb, width), lambda n, r: (n, r, 0)),
        out_shape=jax.ShapeDtypeStruct((n_t, n_chunks, width), F32),
        scratch_shapes=[
            pltpu.VMEM((width, width), BF16),
            pltpu.VMEM((1, ns2), F32),
            pltpu.VMEM((rb, ns2), F32),
            pltpu.VMEM((rb, ns2), F32),
        ],
        compiler_params=_cparams(("parallel", "arbitrary")),
        name="ssm_scan",
    )(x3, bker, w_end, w_fix_t, a_chunk)


def _ssm_post_kernel(y_ref, u_ref, d_ref, gw_ref, gb_ref, bg_ref, o_ref):
    y = y_ref[...] + d_ref[...] * u_ref[...].astype(F32)
    y = jax.nn.gelu(y)
    gate = jnp.dot(y.astype(BF16), gw_ref[...], preferred_element_type=F32) + gb_ref[...]
    y = y * jax.nn.sigmoid(gate)
    ms = jnp.mean(y * y, axis=-1, keepdims=True)
    o_ref[...] = (y * lax.rsqrt(ms + EPS) * bg_ref[...]).astype(o_ref.dtype)


def _ssm_post(y_pre, z, ssm_d, glu_w, glu_b, g_ssm, bt=512):
    L = y_pre.shape[0]
    row = lambda v: v.reshape(1, C_SSM)
    vec = pl.BlockSpec((1, C_SSM), lambda i: (0, 0))
    return pl.pallas_call(
        _ssm_post_kernel,
        grid=(L // bt,),
        in_specs=[
            pl.BlockSpec((bt, C_SSM), lambda i: (i, 0)),
            pl.BlockSpec((bt, C_SSM), lambda i: (i, Z_SSM // C_SSM)),
            vec,
            pl.BlockSpec((C_SSM, C_SSM), lambda i: (0, 0)),
            vec, vec,
        ],
        out_specs=pl.BlockSpec((bt, C_SSM), lambda i: (i, 0)),
        out_shape=jax.ShapeDtypeStruct((L, C_SSM), BF16),
        compiler_params=_cparams(("parallel",)),
        name="ssm_post",
    )(y_pre, z, row(ssm_d), glu_w.astype(BF16), row(glu_b), row(g_ssm))


def _s5_module(z, A_re, A_im, log_dt, B_re, B_im, C_re, C_im, ssm_d, glu_w, glu_b, g_ssm):
    L = z.shape[0]
    T = SSM_CHUNK
    ops = _ssm_prep(A_re, A_im, log_dt, B_re, B_im, C_re, C_im)
    u = z[:, Z_SSM:Z_SSM + C_SSM]
    x3 = u.reshape(L // T, T, SSM_LANE_GROUPS, LANES).transpose(2, 0, 1, 3)
    x3 = x3.reshape(SSM_LANE_GROUPS, L // T, T * LANES)
    y3 = _ssm_scan(x3, *ops)
    y_pre = y3.reshape(SSM_LANE_GROUPS, L // T, T, LANES).transpose(1, 2, 0, 3).reshape(L, C_SSM)
    return _ssm_post(y_pre, z, ssm_d, glu_w, glu_b, g_ssm)


BIAS_COLS = 6
LOG2E = 1.4426950408889634


def _attn_kernel(qi_ref, kj_ref, ref_ref, q_ref, aq_ref, k_ref, ak_ref, v_ref, o_ref,
                 qq_scr, m_scr, acc_scr):
    h = pl.program_id(0)
    p = pl.program_id(1)
    i = qi_ref[p]
    j = kj_ref[p]
    bq, bk = q_ref.shape[0], k_ref.shape[0]

    @pl.when(j == 0)
    def _():
        qq_scr[:, 0:HEAD_DIM] = q_ref[...]
        qq_scr[:, HEAD_DIM:2 * HEAD_DIM] = aq_ref[...]
        m_scr[...] = jnp.full_like(m_scr, NEG_BIG)
        acc_scr[...] = jnp.zeros_like(acc_scr)

    lane = lax.broadcasted_iota(jnp.int32, (bk, HEAD_DIM), 1)
    mine = jnp.logical_and(lane >= BIAS_COLS * h, lane < BIAS_COLS * (h + 1))
    ak = jnp.where(mine, ak_ref[...], jnp.zeros_like(ak_ref[...]))
    kk = jnp.concatenate([k_ref[...], ak], axis=1)
    ones_col = jnp.where(lane == 0, 1.0, 0.0).astype(BF16)
    vv = jnp.concatenate([v_ref[...], ones_col], axis=1)
    s = lax.dot_general(qq_scr[...], kk, (((1,), (1,)), ((), ())),
                        preferred_element_type=F32)
    n_blk = ref_ref.shape[0] // N_HEADS
    shift = ref_ref[h * n_blk + i] - ref_ref[h * n_blk + j]

    def update(s):
        m_prev = m_scr[...]
        m_new = jnp.maximum(m_prev, jnp.max(s, axis=1, keepdims=True) + shift)
        alpha = jnp.exp2(m_prev - m_new)
        pr = jnp.exp2(s - (m_new - shift)).astype(BF16)
        acc_scr[...] = alpha * acc_scr[...] + jnp.dot(pr, vv, preferred_element_type=F32)
        m_scr[...] = m_new

    @pl.when(j < i)
    def _():
        update(s)

    @pl.when(j == i)
    def _():
        rows = lax.broadcasted_iota(jnp.int32, (bq, bk), 0)
        cols = lax.broadcasted_iota(jnp.int32, (bq, bk), 1)
        update(jnp.where(cols <= rows, s, NEG_BIG))
        acc = acc_scr[...]
        o_ref[...] = (acc[:, 0:HEAD_DIM] / acc[:, HEAD_DIM:HEAD_DIM + 1]).astype(o_ref.dtype)


def _fox_attention(z, c, blk=2048):
    L = z.shape[0]
    blk = min(blk, L)
    nb = L // blk
    qi = np.repeat(np.arange(nb), np.arange(1, nb + 1)).astype(np.int32)
    kj = np.concatenate([np.arange(i + 1) for i in range(nb)]).astype(np.int32)

    c2 = (c * LOG2E).reshape(N_HEADS, nb, blk)
    ref = c2[:, :, 0]
    d = (c2 - ref[:, :, None]).reshape(N_HEADS, L).T

    def bf16_part(v):
        bits = lax.bitcast_convert_type(v, jnp.uint32) & jnp.uint32(0xFFFF0000)
        return lax.bitcast_convert_type(bits, F32)

    p1 = bf16_part(d)
    p2 = bf16_part(d - p1)
    p3 = bf16_part(d - p1 - p2)
    d1, d2, d3 = p1.astype(BF16), p2.astype(BF16), p3.astype(BF16)
    one = jnp.ones_like(d1)
    pad = jnp.zeros((L, HEAD_DIM - N_HEADS * BIAS_COLS), BF16)
    aq = jnp.concatenate(
        [jnp.stack([d1, d2, d3, one, one, one], axis=-1).reshape(L, -1), pad], axis=1)
    ak = jnp.concatenate(
        [jnp.stack([one, one, one, -d1, -d2, -d3], axis=-1).reshape(L, -1), pad], axis=1)

    q0, k0, v0 = Z_Q // HEAD_DIM, Z_K // HEAD_DIM, Z_V // HEAD_DIM
    grid_spec = pltpu.PrefetchScalarGridSpec(
        num_scalar_prefetch=3,
        grid=(N_HEADS, len(qi)),
        in_specs=[
            pl.BlockSpec((blk, HEAD_DIM), lambda h, p, qi, kj, rf: (qi[p], q0 + h)),
            pl.BlockSpec((blk, HEAD_DIM), lambda h, p, qi, kj, rf: (qi[p], 0)),
            pl.BlockSpec((blk, HEAD_DIM), lambda h, p, qi, kj, rf: (kj[p], k0 + h)),
            pl.BlockSpec((blk, HEAD_DIM), lambda h, p, qi, kj, rf: (kj[p], 0)),
            pl.BlockSpec((blk, HEAD_DIM), lambda h, p, qi, kj, rf: (kj[p], v0 + h)),
        ],
        out_specs=pl.BlockSpec((blk, HEAD_DIM), lambda h, p, qi, kj, rf: (qi[p], h)),
        scratch_shapes=[
            pltpu.VMEM((blk, 2 * HEAD_DIM), BF16),
            pltpu.VMEM((blk, 1), F32),
            pltpu.VMEM((blk, 2 * HEAD_DIM), F32),
        ],
    )
    return pl.pallas_call(
        _attn_kernel,
        grid_spec=grid_spec,
        out_shape=jax.ShapeDtypeStruct((L, D_ATT), BF16),
        compiler_params=_cparams(("parallel", "arbitrary")),
        name="fox_attention",
    )(jnp.asarray(qi), jnp.asarray(kj), ref.reshape(-1), z, aq, z, ak, z)


def _outproj_kernel(yc_ref, ys_ref, ya_ref, h_ref, wc_ref, ws_ref, wa_ref, ga_ref,
                    nf_ref, rw_ref, rb_ref, ho_ref, hn_ref, idx_ref, gate_ref):
    ya = ya_ref[...].astype(F32)
    ms = jnp.mean(ya * ya, axis=-1, keepdims=True)
    ya = (ya * lax.rsqrt(ms + EPS) * ga_ref[...]).astype(BF16)
    mix = jnp.dot(yc_ref[...], wc_ref[...], preferred_element_type=F32)
    mix = mix + jnp.dot(ys_ref[...], ws_ref[...], preferred_element_type=F32)
    mix = mix + jnp.dot(ya, wa_ref[...], preferred_element_type=F32)
    h = h_ref[...] + mix
    ho_ref[...] = h

    ms = jnp.mean(h * h, axis=-1, keepdims=True)
    hn = h * lax.rsqrt(ms + EPS) * nf_ref[...]
    hn_ref[...] = hn.astype(hn_ref.dtype)
    logits = jnp.dot(hn, rw_ref[...], preferred_element_type=F32,
                     precision=lax.Precision.HIGHEST) + rb_ref[...]
    col = lax.broadcasted_iota(jnp.int32, logits.shape, 1)
    kcol = lax.broadcasted_iota(jnp.int32, idx_ref.shape, 1)
    idx_out = jnp.zeros(idx_ref.shape, jnp.int32)
    val_out = jnp.zeros(gate_ref.shape, F32)
    top = None
    for k in range(TOP_K):
        m = jnp.max(logits, axis=1, keepdims=True)
        sel = jnp.min(jnp.where(logits == m, col, N_EXPERTS), axis=1, keepdims=True)
        if top is None:
            top = m
        idx_out = jnp.where(kcol == k, sel, idx_out)
        val_out = jnp.where(kcol == k, jnp.exp(m - top), val_out)
        logits = jnp.where(col == sel, -jnp.inf, logits)
    idx_ref[...] = idx_out
    gate_ref[...] = val_out / jnp.sum(val_out, axis=1, keepdims=True)


def _outproj_router(yc, ys, ya, h, w_out, g_att, norm_ffn, router_w, router_b, bm=512):
    L, D = h.shape
    full = lambda shape: pl.BlockSpec(shape, lambda i: (0,) * len(shape))
    return pl.pallas_call(
        _outproj_kernel,
        grid=(L // bm,),
        in_specs=[
            pl.BlockSpec((bm, C_CONV), lambda i: (i, 0)),
            pl.BlockSpec((bm, C_SSM), lambda i: (i, 0)),
            pl.BlockSpec((bm, D_ATT), lambda i: (i, 0)),
            pl.BlockSpec((bm, D), lambda i: (i, 0)),
            pl.BlockSpec((C_CONV, D), lambda i: (0, 0)),
            pl.BlockSpec((C_SSM, D), lambda i: (1, 0)),
            pl.BlockSpec((D_ATT, D), lambda i: (1, 0)),
            full((1, D_ATT)),
            full((1, D)),
            full((D, N_EXPERTS)),
            full((1, N_EXPERTS)),
        ],
        out_specs=[
            pl.BlockSpec((bm, D), lambda i: (i, 0)),
            pl.BlockSpec((bm, D), lambda i: (i, 0)),
            pl.BlockSpec((bm, TOP_K), lambda i: (i, 0)),
            pl.BlockSpec((bm, TOP_K), lambda i: (i, 0)),
        ],
        out_shape=[
            jax.ShapeDtypeStruct((L, D), F32),
            jax.ShapeDtypeStruct((L, D), BF16),
            jax.ShapeDtypeStruct((L, TOP_K), jnp.int32),
            jax.ShapeDtypeStruct((L, TOP_K), F32),
        ],
        compiler_params=_cparams(("parallel",)),
        name="outproj_router",
    )(yc, ys, ya, h, w_out, w_out, w_out, g_att.reshape(1, D_ATT),
      norm_ffn.reshape(1, D), router_w, router_b.reshape(1, N_EXPERTS))


GU_GROUP = 2 * LANES
MOE_CHUNKS = 2


def _gu_perm():
    src = np.arange(GU_GROUP)
    dst = np.where(src % 2 == 0, src // 2, LANES + src // 2)
    p = np.zeros((GU_GROUP, GU_GROUP), np.float32)
    p[src, dst] = 1.0
    return jnp.asarray(p, BF16)


def _wprep_kernel(w_ref, p_ref, o_ref):
    perm = p_ref[...]
    for c in range(w_ref.shape[1] // GU_GROUP):
        blk = w_ref[:, c * GU_GROUP:(c + 1) * GU_GROUP].astype(BF16)
        o_ref[0, :, c * GU_GROUP:(c + 1) * GU_GROUP] = jnp.dot(
            blk, perm, preferred_element_type=F32).astype(BF16)


def _prep_gate_up(w_gu_all, layer, br=1024, bc=1024):
    _, E, D, N = w_gu_all.shape
    return pl.pallas_call(
        _wprep_kernel,
        grid=(E, D // br, N // bc),
        in_specs=[
            pl.BlockSpec((None, None, br, bc), lambda e, i, j: (layer, e, i, j)),
            pl.BlockSpec((GU_GROUP, GU_GROUP), lambda e, i, j: (0, 0)),
        ],
        out_specs=pl.BlockSpec((1, br, bc), lambda e, i, j: (e, i, j)),
        out_shape=jax.ShapeDtypeStruct((E, D, N), BF16),
        compiler_params=_cparams(("parallel", "parallel", "parallel")),
        name="prep_gate_up",
    )(w_gu_all, _gu_perm())


def _moe_kernel(be_ref, nu_ref, x_ref, wgu_ref, bgu_ref, wdn_ref, bdn_ref, g_ref, *rest):
    o_ref, acc_ref = rest[-2:]
    b = pl.program_id(0)
    j = pl.program_id(1)
    nj = pl.num_programs(1)
    used = b < nu_ref[0]

    @pl.when(jnp.logical_and(jnp.logical_not(used), j == nj - 1))
    def _():
        o_ref[...] = jnp.zeros_like(o_ref)

    @pl.when(used)
    def _():
        @pl.when(j == 0)
        def _():
            acc_ref[...] = jnp.zeros_like(acc_ref)

        hgu = jnp.dot(x_ref[...], wgu_ref[0], preferred_element_type=F32) + bgu_ref[0]
        parts = []
        for c in range(hgu.shape[1] // GU_GROUP):
            g = jnp.minimum(hgu[:, c * GU_GROUP:c * GU_GROUP + LANES], SWIGLU_LIMIT)
            u = jnp.clip(hgu[:, c * GU_GROUP + LANES:(c + 1) * GU_GROUP],
                         -SWIGLU_LIMIT, SWIGLU_LIMIT)
            parts.append((g * jax.nn.sigmoid(SWIGLU_ALPHA * g) * (u + 1.0)).astype(BF16))
        a = jnp.concatenate(parts, axis=1)
        acc_ref[...] += jnp.dot(a, wdn_ref[...].astype(BF16), preferred_element_type=F32)

        @pl.when(j == nj - 1)
        def _():
            o_ref[...] = ((acc_ref[...] + bdn_ref[0]) * g_ref[...]).astype(o_ref.dtype)


def _moe_experts(xs, block_e, n_used, wgu_p, bgu_p, w_dn_all, layer, b_dn, row_gate, bm, tf=512,
                 first_block=0, total_rows=None, ys_prev=None):
    rows, D = xs.shape
    R = total_rows or rows
    _, E, d_ff, _ = w_dn_all.shape
    nb = rows // bm
    nj = d_ff // tf

    def blk(b, nu):
        return jnp.maximum(jnp.minimum(b, nu[0] - 1), 0)

    args = [block_e, n_used, xs, wgu_p, bgu_p.reshape(E, 1, 2 * d_ff), w_dn_all,
            b_dn.reshape(E, 1, D), row_gate]
    extra_specs, aliases = [], {}
    if ys_prev is not None:
        extra_specs = [pl.BlockSpec(memory_space=pl.ANY)]
        aliases = {len(args): 0}
        args.append(ys_prev)

    grid_spec = pltpu.PrefetchScalarGridSpec(
        num_scalar_prefetch=2,
        grid=(nb, nj),
        in_specs=[
            pl.BlockSpec((bm, D), lambda b, j, be, nu: (blk(b, nu), 0)),
            pl.BlockSpec((1, D, 2 * tf), lambda b, j, be, nu: (be[blk(b, nu)], 0, j)),
            pl.BlockSpec((1, 1, 2 * tf), lambda b, j, be, nu: (be[blk(b, nu)], 0, j)),
            pl.BlockSpec((None, None, tf, D),
                         lambda b, j, be, nu: (layer, be[blk(b, nu)], j, 0)),
            pl.BlockSpec((1, 1, D), lambda b, j, be, nu: (be[blk(b, nu)], 0, 0)),
            pl.BlockSpec((bm, 1), lambda b, j, be, nu: (blk(b, nu), 0)),
        ] + extra_specs,
        out_specs=pl.BlockSpec((bm, D), lambda b, j, be, nu: (b + first_block, 0)),
        scratch_shapes=[pltpu.VMEM((bm, D), F32)],
    )
    return pl.pallas_call(
        _moe_kernel,
        grid_spec=grid_spec,
        out_shape=jax.ShapeDtypeStruct((R, D), F32),
        input_output_aliases=aliases,
        compiler_params=_cparams(("arbitrary", "arbitrary")),
        name="moe_experts",
    )(*args)


def _moe(h, hn, top_idx, gates, w_gu_all, layer, b_gu, w_dn_all, b_dn, bm=512):
    T, D = h.shape
    E = N_EXPERTS
    n_rows = T * TOP_K
    flat_e = top_idx.reshape(-1)
    flat_g = gates.reshape(-1)
    order = jnp.argsort(flat_e, stable=True).astype(jnp.int32)
    rank = jnp.argsort(order).astype(jnp.int32)
    counts = jnp.sum((flat_e[None, :] == jnp.arange(E)[:, None]).astype(jnp.int32), axis=1)
    padded = (counts + bm - 1) // bm * bm
    start = jnp.cumsum(counts) - counts
    pend = jnp.cumsum(padded)
    pstart = pend - padded
    n_blocks = -(-n_rows // bm) + E
    R = n_blocks * bm
    block_e = jnp.minimum(
        jnp.sum((pend[None, :] <= (jnp.arange(n_blocks) * bm)[:, None]).astype(jnp.int32), axis=1),
        E - 1).astype(jnp.int32)
    n_used = (pend[-1:] // bm).astype(jnp.int32)
    row_e = jnp.repeat(block_e, bm)
    within = jnp.arange(R, dtype=jnp.int32) - pstart[row_e].astype(jnp.int32)
    valid = within < counts[row_e]
    src = order[jnp.clip(start[row_e].astype(jnp.int32) + within, 0, n_rows - 1)]
    buf_tok = jnp.where(valid, src // TOP_K, 0)
    buf_g = jnp.where(valid, flat_g[src], 0.0)
    pos = (rank + (pstart - start).astype(jnp.int32)[flat_e]).reshape(T, TOP_K)

    wgu_p = _prep_gate_up(w_gu_all, layer)
    bgu_p = b_gu.reshape(E, -1, LANES, 2).transpose(0, 1, 3, 2).reshape(E, -1)
    cb = n_blocks // MOE_CHUNKS
    ys = None
    for c in range(MOE_CHUNKS):
        rows = slice(c * cb * bm, (c + 1) * cb * bm)
        ys = _moe_experts(
            hn[buf_tok[rows]], block_e[c * cb:(c + 1) * cb], jnp.clip(n_used - c * cb, 0, cb),
            wgu_p, bgu_p, w_dn_all, layer, b_dn, buf_g[rows].reshape(-1, 1), bm,
            first_block=c * cb, total_rows=R, ys_prev=ys)
    return _moe_combine(h, ys[pos.T])


def _combine_kernel(h_ref, y_ref, o_ref):
    acc = h_ref[...]
    for k in range(TOP_K):
        acc = acc + y_ref[k]
    o_ref[...] = acc


def _moe_combine(h, y4, tb=256):
    T, D = h.shape
    return pl.pallas_call(
        _combine_kernel,
        grid=(T // tb,),
        in_specs=[
            pl.BlockSpec((tb, D), lambda i: (i, 0)),
            pl.BlockSpec((TOP_K, tb, D), lambda i: (0, i, 0)),
        ],
        out_specs=pl.BlockSpec((tb, D), lambda i: (i, 0)),
        out_shape=jax.ShapeDtypeStruct((T, D), F32),
        compiler_params=_cparams(("parallel",)),
        name="moe_combine",
    )(h, y4)


def _layer(h, p, layer, w_gu_all, w_dn_all):
    w_in = p['w_in']
    scale = LOG2E * HEAD_DIM ** -0.5
    z, f_t = _inproj(
        h, p['norm_mix'].reshape(1, -1),
        w_in[:, :Z_WIDTH].astype(BF16),
        w_in[:, Z_WIDTH:].T.astype(BF16),
        (p['q_norm'] * scale).reshape(1, HEAD_DIM), p['k_norm'].reshape(1, HEAD_DIM))
    c = _cumgate(f_t, p['b_forget'])
    g_conv = p['branch_norm'][:C_CONV]
    g_ssm = p['branch_norm'][C_CONV:C_CONV + C_SSM]
    g_att = p['branch_norm'][C_CONV + C_SSM:]
    yc = _conv_module(z, p['conv_w'], p['conv_b'], p['conv_ln_g'], p['conv_ln_b'], g_conv)
    ys = _s5_module(z, p['ssm_A_re'], p['ssm_A_im'], p['ssm_log_dt'], p['ssm_B_re'],
                    p['ssm_B_im'], p['ssm_C_re'], p['ssm_C_im'], p['ssm_D'],
                    p['ssm_glu_w'], p['ssm_glu_b'], g_ssm)
    ya = _fox_attention(z, c)
    h, hn, top_idx, gates = _outproj_router(
        yc, ys, ya, h, p['w_out'].astype(BF16), g_att, p['norm_ffn'],
        p['router_w'], p['router_b'])
    return _moe(h, hn, top_idx, gates, w_gu_all, layer, p['b_gate_up'], w_dn_all, p['b_down'])


def kernel(x, norm_mix, w_in, b_forget, conv_w, conv_b, conv_ln_g, conv_ln_b, ssm_A_re,
           ssm_A_im, ssm_log_dt, ssm_B_re, ssm_B_im, ssm_C_re, ssm_C_im, ssm_D, ssm_glu_w,
           ssm_glu_b, q_norm, k_norm, branch_norm, w_out, norm_ffn, router_w, router_b,
           w_gate_up, b_gate_up, w_down, b_down):
    params = dict(
        norm_mix=norm_mix, w_in=w_in, b_forget=b_forget, conv_w=conv_w, conv_b=conv_b,
        conv_ln_g=conv_ln_g, conv_ln_b=conv_ln_b, ssm_A_re=ssm_A_re, ssm_A_im=ssm_A_im,
        ssm_log_dt=ssm_log_dt, ssm_B_re=ssm_B_re, ssm_B_im=ssm_B_im, ssm_C_re=ssm_C_re,
        ssm_C_im=ssm_C_im, ssm_D=ssm_D, ssm_glu_w=ssm_glu_w, ssm_glu_b=ssm_glu_b,
        q_norm=q_norm, k_norm=k_norm, branch_norm=branch_norm, w_out=w_out,
        norm_ffn=norm_ffn, router_w=router_w, router_b=router_b,
        b_gate_up=b_gate_up, b_down=b_down)
    bsz, L, D = x.shape
    assert bsz == 1
    h = x.reshape(L, D)
    for l in range(norm_mix.shape[0]):
        h = _layer(h, {k: v[l] for k, v in params.items()}, l, w_gate_up, w_down)
    return h.reshape(bsz, L, D)
```

```python
import functools
import math

import jax
import jax.numpy as jnp
import numpy as np
from jax import lax
from jax.experimental import pallas as pl
from jax.experimental.pallas import tpu as pltpu

F32 = jnp.float32
BF16 = jnp.bfloat16

C_CONV = 512
C_SSM = 512
D_ATT = 1024
HEAD_DIM = 128
N_HEADS = 8
CONV_WIDTH = 31
SSM_GROUP_CH = 16
SSM_GROUPS = 32
SSM_STATE = 64
N_EXPERTS = 32
TOP_K = 4
SWIGLU_LIMIT = 7.0
SWIGLU_ALPHA = 1.702
EPS = 1e-6

LANES = 128
VMEM_LIMIT = 56 * 1024 * 1024

Z_CONV = 0
Z_SSM = 2 * C_CONV
Z_Q = Z_SSM + C_SSM
Z_K = Z_Q + D_ATT
Z_V = Z_K + D_ATT
Z_WIDTH = Z_V + D_ATT
IN_TILE = 512

SSM_CHUNK = 16
SSM_LANE_GROUPS = C_SSM // LANES
SSM_GPL = LANES // SSM_GROUP_CH

NEG_BIG = -1e30


def _cparams(sem, vmem=VMEM_LIMIT):
    return pltpu.CompilerParams(dimension_semantics=sem, vmem_limit_bytes=vmem)


def _inproj_kernel(x_ref, g_ref, w_ref, wf_ref, qg_ref, kg_ref, z_ref, f_ref, hn_ref):
    j = pl.program_id(1)

    @pl.when(j == 0)
    def _():
        x = x_ref[...]
        ms = jnp.mean(x * x, axis=-1, keepdims=True)
        hn = (x * lax.rsqrt(ms + EPS) * g_ref[...]).astype(BF16)
        hn_ref[...] = hn
        f_ref[...] = lax.dot_general(wf_ref[...], hn, (((1,), (1,)), ((), ())),
                                     preferred_element_type=F32)

    acc = jnp.dot(hn_ref[...], w_ref[...], preferred_element_type=F32)
    q_lo, k_lo, v_lo = Z_Q // IN_TILE, Z_K // IN_TILE, Z_V // IN_TILE
    is_q = jnp.logical_and(j >= q_lo, j < k_lo)
    is_k = jnp.logical_and(j >= k_lo, j < v_lo)

    def headnorm(gain):
        for hh in range(IN_TILE // HEAD_DIM):
            a = acc[:, hh * HEAD_DIM:(hh + 1) * HEAD_DIM]
            ms = jnp.mean(a * a, axis=-1, keepdims=True)
            z_ref[:, hh * HEAD_DIM:(hh + 1) * HEAD_DIM] = (
                a * lax.rsqrt(ms + EPS) * gain).astype(z_ref.dtype)

    @pl.when(is_q)
    def _():
        headnorm(qg_ref[...])

    @pl.when(is_k)
    def _():
        headnorm(kg_ref[...])

    @pl.when(jnp.logical_not(jnp.logical_or(is_q, is_k)))
    def _():
        z_ref[...] = acc.astype(z_ref.dtype)


def _inproj(h, g, w_main, w_f, q_gain, k_gain, bm=1024):
    L, D = h.shape
    n_tiles = Z_WIDTH // IN_TILE
    return pl.pallas_call(
        _inproj_kernel,
        grid=(L // bm, n_tiles),
        in_specs=[
            pl.BlockSpec((bm, D), lambda i, j: (i, 0)),
            pl.BlockSpec((1, D), lambda i, j: (0, 0)),
            pl.BlockSpec((D, IN_TILE), lambda i, j: (0, j)),
            pl.BlockSpec((N_HEADS, D), lambda i, j: (0, 0)),
            pl.BlockSpec((1, HEAD_DIM), lambda i, j: (0, 0)),
            pl.BlockSpec((1, HEAD_DIM), lambda i, j: (0, 0)),
        ],
        out_specs=[
            pl.BlockSpec((bm, IN_TILE), lambda i, j: (i, j)),
            pl.BlockSpec((N_HEADS, bm), lambda i, j: (0, i)),
        ],
        out_shape=[
            jax.ShapeDtypeStruct((L, Z_WIDTH), BF16),
            jax.ShapeDtypeStruct((N_HEADS, L), F32),
        ],
        scratch_shapes=[pltpu.VMEM((bm, D), BF16)],
        compiler_params=_cparams(("parallel", "arbitrary")),
        name="inproj",
    )(h, g, w_main, w_f, q_gain, k_gain)


def _cumgate_kernel(f_ref, b_ref, c_ref):
    x = jax.nn.log_sigmoid(f_ref[...] + b_ref[...])
    n = x.shape[1]
    pos = lax.broadcasted_iota(jnp.int32, x.shape, 1)
    d = 1
    while d < n:
        x = x + jnp.where(pos >= d, pltpu.roll(x, d, axis=1), 0.0)
        d *= 2
    c_ref[...] = x


def _cumgate(f_t, b_forget):
    H, L = f_t.shape
    return pl.pallas_call(
        _cumgate_kernel,
        out_shape=jax.ShapeDtypeStruct((H, L), F32),
        compiler_params=pltpu.CompilerParams(vmem_limit_bytes=VMEM_LIMIT),
        name="cumgate",
    )(f_t, b_forget.reshape(H, 1))


CONV_HALO = 32


def _conv_kernel(a_ref, gate_ref, cw_ref, cb_ref, lg_ref, lb_ref, bg_ref, o_ref, ubuf):
    i = pl.program_id(0)
    bt = a_ref.shape[0]

    @pl.when(i == 0)
    def _():
        ubuf[0:CONV_HALO, :] = jnp.zeros((CONV_HALO, C_CONV), F32)

    @pl.when(i > 0)
    def _():
        ubuf[0:CONV_HALO, :] = ubuf[bt:bt + CONV_HALO, :]

    a = a_ref[...].astype(F32)
    gate = gate_ref[...].astype(F32)
    ubuf[CONV_HALO:CONV_HALO + bt, :] = a * jax.nn.sigmoid(gate)

    base = CONV_HALO - (CONV_WIDTH - 1)
    acc = jnp.zeros((bt, C_CONV), F32)
    for w in range(CONV_WIDTH):
        acc = acc + ubuf[base + w:base + w + bt, :] * cw_ref[w:w + 1, :]
    y = acc + cb_ref[...]
    mu = jnp.mean(y, axis=-1, keepdims=True)
    yc = y - mu
    var = jnp.mean(yc * yc, axis=-1, keepdims=True)
    y = yc * lax.rsqrt(var + EPS) * lg_ref[...] + lb_ref[...]
    y = y * jax.nn.sigmoid(y)
    ms = jnp.mean(y * y, axis=-1, keepdims=True)
    o_ref[...] = (y * lax.rsqrt(ms + EPS) * bg_ref[...]).astype(o_ref.dtype)


def _conv_module(z, conv_w, conv_b, ln_g, ln_b, g_conv, bt=512):
    L = z.shape[0]
    row = lambda v: v.reshape(1, C_CONV)
    vec = pl.BlockSpec((1, C_CONV), lambda i: (0, 0))
    return pl.pallas_call(
        _conv_kernel,
        grid=(L // bt,),
        in_specs=[
            pl.BlockSpec((bt, C_CONV), lambda i: (i, Z_CONV // C_CONV)),
            pl.BlockSpec((bt, C_CONV), lambda i: (i, Z_CONV // C_CONV + 1)),
            pl.BlockSpec((CONV_WIDTH, C_CONV), lambda i: (0, 0)),
            vec, vec, vec, vec,
        ],
        out_specs=pl.BlockSpec((bt, C_CONV), lambda i: (i, 0)),
        out_shape=jax.ShapeDtypeStruct((L, C_CONV), BF16),
        scratch_shapes=[pltpu.VMEM((bt + CONV_HALO, C_CONV), F32)],
        compiler_params=_cparams(("arbitrary",)),
        name="conv_module",
    )(z, z, conv_w, row(conv_b), row(ln_g), row(ln_b), row(g_conv))


def _ssm_prep_kernel(par_ref, btr_ref, bti_ref, ctr_ref, cti_ref, we_ref, wf_ref, bk_ref, ac_ref):
    T = SSM_CHUNK
    a_re = par_ref[0, 0:1, :]
    a_im = par_ref[0, 1:2, :]
    dt = jnp.exp(par_ref[0, 2:3, :])
    shape = btr_ref.shape[1:]
    same_group = (lax.broadcasted_iota(jnp.int32, shape, 0) // SSM_GROUP_CH
                  == lax.broadcasted_iota(jnp.int32, shape, 1) // SSM_STATE)

    def power(j):
        mag = jnp.exp(float(j) * (dt * a_re))
        ang = float(j) * (dt * a_im)
        return mag * jnp.cos(ang), mag * jnp.sin(ang)

    ab_re, ab_im = power(1)
    inv = 1.0 / (a_re * a_re + a_im * a_im)
    ir, ii = a_re * inv, -a_im * inv
    dr, di = ab_re - 1.0, ab_im
    f_re, f_im = dr * ir - di * ii, dr * ii + di * ir
    btr, bti = btr_ref[0], bti_ref[0]
    bb_re = jnp.where(same_group, f_re * btr - f_im * bti, 0.0)
    bb_im = jnp.where(same_group, f_re * bti + f_im * btr, 0.0)
    c_re = jnp.where(same_group, ctr_ref[0], 0.0)
    c_im = jnp.where(same_group, cti_ref[0], 0.0)
    c0 = jnp.concatenate([c_re, -c_im], axis=1)

    for j in range(T + 1):
        p_re, p_im = power(j)
        if j < T:
            ebar = jnp.concatenate([p_re * bb_re - p_im * bb_im,
                                    p_re * bb_im + p_im * bb_re], axis=1)
            we_ref[0, T - 1 - j] = ebar.astype(we_ref.dtype)
            bk_ref[0, j] = lax.dot_general(
                ebar, c0, (((1,), (1,)), ((), ())), preferred_element_type=F32,
                precision=lax.Precision.HIGHEST).astype(bk_ref.dtype)
        if j >= 1:
            wf_ref[0, j - 1] = jnp.concatenate(
                [c_re * p_re - c_im * p_im, -(c_re * p_im + c_im * p_re)],
                axis=1).astype(wf_ref.dtype)
        if j == T:
            ac_ref[0] = jnp.concatenate([p_re, p_im], axis=1)


def _ssm_prep(A_re, A_im, log_dt, B_re, B_im, C_re, C_im):
    n_t, gpl, H, P, T = SSM_LANE_GROUPS, SSM_GPL, SSM_GROUP_CH, SSM_STATE, SSM_CHUNK
    ns = gpl * P
    par = jnp.stack([A_re.reshape(n_t, ns), A_im.reshape(n_t, ns),
                     jnp.repeat(log_dt, P).reshape(n_t, ns)], axis=1).astype(F32)

    def b_tile(b):
        bt = b.astype(F32).reshape(n_t, gpl, P, H).transpose(0, 1, 3, 2).reshape(n_t, LANES, P)
        return jnp.tile(bt, (1, 1, gpl))

    def c_tile(c):
        return jnp.tile(c.astype(F32).reshape(n_t, LANES, P), (1, 1, gpl))

    tile_spec = pl.BlockSpec((1, LANES, ns), lambda n: (n, 0, 0))
    op_spec = pl.BlockSpec((1, T, LANES, 2 * ns), lambda n: (n, 0, 0, 0))
    w_end, w_fix_t, bker, a_chunk = pl.pallas_call(
        _ssm_prep_kernel,
        grid=(n_t,),
        in_specs=[pl.BlockSpec((1, 3, ns), lambda n: (n, 0, 0)),
                  tile_spec, tile_spec, tile_spec, tile_spec],
        out_specs=[op_spec, op_spec,
                   pl.BlockSpec((1, T, LANES, LANES), lambda n: (n, 0, 0, 0)),
                   pl.BlockSpec((1, 1, 2 * ns), lambda n: (n, 0, 0))],
        out_shape=[jax.ShapeDtypeStruct((n_t, T, LANES, 2 * ns), BF16),
                   jax.ShapeDtypeStruct((n_t, T, LANES, 2 * ns), BF16),
                   jax.ShapeDtypeStruct((n_t, T, LANES, LANES), BF16),
                   jax.ShapeDtypeStruct((n_t, 1, 2 * ns), F32)],
        compiler_params=_cparams(("parallel",)),
        name="ssm_prep",
    )(par, b_tile(B_re), b_tile(B_im), c_tile(C_re), c_tile(C_im))
    return (bker, w_end.reshape(n_t, T * LANES, 2 * ns),
            w_fix_t.reshape(n_t, T * LANES, 2 * ns), a_chunk)


def _ssm_scan_kernel(x_ref, bk_ref, we_ref, wf_ref, a_ref, y_ref, wt_scr, carry, sprev, e_scr):
    r = pl.program_id(1)
    rb = x_ref.shape[1]
    ns = carry.shape[1] // 2
    T = SSM_CHUNK

    @pl.when(r == 0)
    def _():
        carry[...] = jnp.zeros_like(carry)
        for s in range(T):
            for t in range(T):
                blk = bk_ref[0, t - s] if t >= s else jnp.zeros((LANES, LANES), wt_scr.dtype)
                wt_scr[s * LANES:(s + 1) * LANES, t * LANES:(t + 1) * LANES] = blk

    x = x_ref[0]
    y_local = jnp.dot(x, wt_scr[...], preferred_element_type=F32)
    e_scr[...] = jnp.dot(x, we_ref[0], preferred_element_type=F32)
    a_re = a_ref[0, :, 0:ns]
    a_im = a_ref[0, :, ns:2 * ns]

    def step(k, s):
        s_re, s_im = s
        sprev[pl.ds(k, 1), 0:ns] = s_re
        sprev[pl.ds(k, 1), ns:2 * ns] = s_im
        e_re = e_scr[pl.ds(k, 1), 0:ns]
        e_im = e_scr[pl.ds(k, 1), ns:2 * ns]
        return (a_re * s_re - a_im * s_im + e_re, a_re * s_im + a_im * s_re + e_im)

    s_re, s_im = lax.fori_loop(0, rb, step, (carry[:, 0:ns], carry[:, ns:2 * ns]))
    carry[:, 0:ns] = s_re
    carry[:, ns:2 * ns] = s_im
    y_fix = lax.dot_general(sprev[...].astype(BF16), wf_ref[0], (((1,), (1,)), ((), ())),
                            preferred_element_type=F32)
    y_ref[0] = y_local + y_fix


def _ssm_scan(x3, bker, w_end, w_fix_t, a_chunk, rb=256):
    n_t, n_chunks, width = x3.shape
    ns2 = w_end.shape[2]
    rb = min(rb, n_chunks)
    return pl.pallas_call(
        _ssm_scan_kernel,
        grid=(n_t, n_chunks // rb),
        in_specs=[
            pl.BlockSpec((1, rb, width), lambda n, r: (n, r, 0)),
            pl.BlockSpec((1, SSM_CHUNK, LANES, LANES), lambda n, r: (n, 0, 0, 0)),
            pl.BlockSpec((1, width, ns2), lambda n, r: (n, 0, 0)),
            pl.BlockSpec((1, width, ns2), lambda n, r: (n, 0, 0)),
            pl.BlockSpec((1, 1, ns2), lambda n, r: (n, 0, 0)),
        ],
        out_specs=pl.BlockSpec((1, rb, width), lambda n, r: (n, r, 0)),
        out_shape=jax.ShapeDtypeStruct((n_t, n_chunks, width), F32),
        scratch_shapes=[
            pltpu.VMEM((width, width), BF16),
            pltpu.VMEM((1, ns2), F32),
            pltpu.VMEM((rb, ns2), F32),
            pltpu.VMEM((rb, ns2), F32),
        ],
        compiler_params=_cparams(("parallel", "arbitrary")),
        name="ssm_scan",
    )(x3, bker, w_end, w_fix_t, a_chunk)


def _ssm_post_kernel(y_ref, u_ref, d_ref, gw_ref, gb_ref, bg_ref, o_ref):
    y = y_ref[...] + d_ref[...] * u_ref[...].astype(F32)
    y = jax.nn.gelu(y)
    gate = jnp.dot(y.astype(BF16), gw_ref[...], preferred_element_type=F32) + gb_ref[...]
    y = y * jax.nn.sigmoid(gate)
    ms = jnp.mean(y * y, axis=-1, keepdims=True)
    o_ref[...] = (y * lax.rsqrt(ms + EPS) * bg_ref[...]).astype(o_ref.dtype)


def _ssm_post(y_pre, z, ssm_d, glu_w, glu_b, g_ssm, bt=512):
    L = y_pre.shape[0]
    row = lambda v: v.reshape(1, C_SSM)
    vec = pl.BlockSpec((1, C_SSM), lambda i: (0, 0))
    return pl.pallas_call(
        _ssm_post_kernel,
        grid=(L // bt,),
        in_specs=[
            pl.BlockSpec((bt, C_SSM), lambda i: (i, 0)),
            pl.BlockSpec((bt, C_SSM), lambda i: (i, Z_SSM // C_SSM)),
            vec,
            pl.BlockSpec((C_SSM, C_SSM), lambda i: (0, 0)),
            vec, vec,
        ],
        out_specs=pl.BlockSpec((bt, C_SSM), lambda i: (i, 0)),
        out_shape=jax.ShapeDtypeStruct((L, C_SSM), BF16),
        compiler_params=_cparams(("parallel",)),
        name="ssm_post",
    )(y_pre, z, row(ssm_d), glu_w.astype(BF16), row(glu_b), row(g_ssm))


def _s5_module(z, A_re, A_im, log_dt, B_re, B_im, C_re, C_im, ssm_d, glu_w, glu_b, g_ssm):
    L = z.shape[0]
    T = SSM_CHUNK
    ops = _ssm_prep(A_re, A_im, log_dt, B_re, B_im, C_re, C_im)
    u = z[:, Z_SSM:Z_SSM + C_SSM]
    x3 = u.reshape(L // T, T, SSM_LANE_GROUPS, LANES).transpose(2, 0, 1, 3)
    x3 = x3.reshape(SSM_LANE_GROUPS, L // T, T * LANES)
    y3 = _ssm_scan(x3, *ops)
    y_pre = y3.reshape(SSM_LANE_GROUPS, L // T, T, LANES).transpose(1, 2, 0, 3).reshape(L, C_SSM)
    return _ssm_post(y_pre, z, ssm_d, glu_w, glu_b, g_ssm)


BIAS_COLS = 6
LOG2E = 1.4426950408889634


def _attn_kernel(qi_ref, kj_ref, ref_ref, q_ref, aq_ref, k_ref, ak_ref, v_ref, o_ref,
                 qq_scr, m_scr, acc_scr):
    h = pl.program_id(0)
    p = pl.program_id(1)
    i = qi_ref[p]
    j = kj_ref[p]
    bq, bk = q_ref.shape[0], k_ref.shape[0]

    @pl.when(j == 0)
    def _():
        qq_scr[:, 0:HEAD_DIM] = q_ref[...]
        qq_scr[:, HEAD_DIM:2 * HEAD_DIM] = aq_ref[...]
        m_scr[...] = jnp.full_like(m_scr, NEG_BIG)
        acc_scr[...] = jnp.zeros_like(acc_scr)

    lane = lax.broadcasted_iota(jnp.int32, (bk, HEAD_DIM), 1)
    mine = jnp.logical_and(lane >= BIAS_COLS * h, lane < BIAS_COLS * (h + 1))
    ak = jnp.where(mine, ak_ref[...], jnp.zeros_like(ak_ref[...]))
    kk = jnp.concatenate([k_ref[...], ak], axis=1)
    ones_col = jnp.where(lane == 0, 1.0, 0.0).astype(BF16)
    vv = jnp.concatenate([v_ref[...], ones_col], axis=1)
    s = lax.dot_general(qq_scr[...], kk, (((1,), (1,)), ((), ())),
                        preferred_element_type=F32)
    n_blk = ref_ref.shape[0] // N_HEADS
    shift = ref_ref[h * n_blk + i] - ref_ref[h * n_blk + j]

    def update(s):
        m_prev = m_scr[...]
        m_new = jnp.maximum(m_prev, jnp.max(s, axis=1, keepdims=True) + shift)
        alpha = jnp.exp2(m_prev - m_new)
        pr = jnp.exp2(s - (m_new - shift)).astype(BF16)
        acc_scr[...] = alpha * acc_scr[...] + jnp.dot(pr, vv, preferred_element_type=F32)
        m_scr[...] = m_new

    @pl.when(j < i)
    def _():
        update(s)

    @pl.when(j == i)
    def _():
        rows = lax.broadcasted_iota(jnp.int32, (bq, bk), 0)
        cols = lax.broadcasted_iota(jnp.int32, (bq, bk), 1)
        update(jnp.where(cols <= rows, s, NEG_BIG))
        acc = acc_scr[...]
        o_ref[...] = (acc[:, 0:HEAD_DIM] / acc[:, HEAD_DIM:HEAD_DIM + 1]).astype(o_ref.dtype)


def _fox_attention(z, c, blk=2048):
    L = z.shape[0]
    blk = min(blk, L)
    nb = L // blk
    qi = np.repeat(np.arange(nb), np.arange(1, nb + 1)).astype(np.int32)
    kj = np.concatenate([np.arange(i + 1) for i in range(nb)]).astype(np.int32)

    c2 = (c * LOG2E).reshape(N_HEADS, nb, blk)
    ref = c2[:, :, 0]
    d = (c2 - ref[:, :, None]).reshape(N_HEADS, L).T

    def bf16_part(v):
        bits = lax.bitcast_convert_type(v, jnp.uint32) & jnp.uint32(0xFFFF0000)
        return lax.bitcast_convert_type(bits, F32)

    p1 = bf16_part(d)
    p2 = bf16_part(d - p1)
    p3 = bf16_part(d - p1 - p2)
    d1, d2, d3 = p1.astype(BF16), p2.astype(BF16), p3.astype(BF16)
    one = jnp.ones_like(d1)
    pad = jnp.zeros((L, HEAD_DIM - N_HEADS * BIAS_COLS), BF16)
    aq = jnp.concatenate(
        [jnp.stack([d1, d2, d3, one, one, one], axis=-1).reshape(L, -1), pad], axis=1)
    ak = jnp.concatenate(
        [jnp.stack([one, one, one, -d1, -d2, -d3], axis=-1).reshape(L, -1), pad], axis=1)

    q0, k0, v0 = Z_Q // HEAD_DIM, Z_K // HEAD_DIM, Z_V // HEAD_DIM
    grid_spec = pltpu.PrefetchScalarGridSpec(
        num_scalar_prefetch=3,
        grid=(N_HEADS, len(qi)),
        in_specs=[
            pl.BlockSpec((blk, HEAD_DIM), lambda h, p, qi, kj, rf: (qi[p], q0 + h)),
            pl.BlockSpec((blk, HEAD_DIM), lambda h, p, qi, kj, rf: (qi[p], 0)),
            pl.BlockSpec((blk, HEAD_DIM), lambda h, p, qi, kj, rf: (kj[p], k0 + h)),
            pl.BlockSpec((blk, HEAD_DIM), lambda h, p, qi, kj, rf: (kj[p], 0)),
            pl.BlockSpec((blk, HEAD_DIM), lambda h, p, qi, kj, rf: (kj[p], v0 + h)),
        ],
        out_specs=pl.BlockSpec((blk, HEAD_DIM), lambda h, p, qi, kj, rf: (qi[p], h)),
        scratch_shapes=[
            pltpu.VMEM((blk, 2 * HEAD_DIM), BF16),
            pltpu.VMEM((blk, 1), F32),
            pltpu.VMEM((blk, 2 * HEAD_DIM), F32),
        ],
    )
    return pl.pallas_call(
        _attn_kernel,
        grid_spec=grid_spec,
        out_shape=jax.ShapeDtypeStruct((L, D_ATT), BF16),
        compiler_params=_cparams(("parallel", "arbitrary")),
        name="fox_attention",
    )(jnp.asarray(qi), jnp.asarray(kj), ref.reshape(-1), z, aq, z, ak, z)


def _outproj_kernel(yc_ref, ys_ref, ya_ref, h_ref, wc_ref, ws_ref, wa_ref, ga_ref,
                    nf_ref, rw_ref, rb_ref, ho_ref, hn_ref, idx_ref, gate_ref):
    ya = ya_ref[...].astype(F32)
    ms = jnp.mean(ya * ya, axis=-1, keepdims=True)
    ya = (ya * lax.rsqrt(ms + EPS) * ga_ref[...]).astype(BF16)
    mix = jnp.dot(yc_ref[...], wc_ref[...], preferred_element_type=F32)
    mix = mix + jnp.dot(ys_ref[...], ws_ref[...], preferred_element_type=F32)
    mix = mix + jnp.dot(ya, wa_ref[...], preferred_element_type=F32)
    h = h_ref[...] + mix
    ho_ref[...] = h

    ms = jnp.mean(h * h, axis=-1, keepdims=True)
    hn = h * lax.rsqrt(ms + EPS) * nf_ref[...]
    hn_ref[...] = hn.astype(hn_ref.dtype)
    rw = rw_ref[...]
    h1 = hn.astype(BF16)
    h2 = (hn - h1.astype(F32)).astype(BF16)
    w1 = rw.astype(BF16)
    w2 = (rw - w1.astype(F32)).astype(BF16)
    logits = (jnp.dot(h1, w1, preferred_element_type=F32)
              + jnp.dot(h2, w1, preferred_element_type=F32)
              + jnp.dot(h1, w2, preferred_element_type=F32)) + rb_ref[...]
    col = lax.broadcasted_iota(jnp.int32, logits.shape, 1)
    kcol = lax.broadcasted_iota(jnp.int32, idx_ref.shape, 1)
    idx_out = jnp.zeros(idx_ref.shape, jnp.int32)
    val_out = jnp.zeros(gate_ref.shape, F32)
    top = None
    for k in range(TOP_K):
        m = jnp.max(logits, axis=1, keepdims=True)
        sel = jnp.min(jnp.where(logits == m, col, N_EXPERTS), axis=1, keepdims=True)
        if top is None:
            top = m
        idx_out = jnp.where(kcol == k, sel, idx_out)
        val_out = jnp.where(kcol == k, jnp.exp(m - top), val_out)
        logits = jnp.where(col == sel, -jnp.inf, logits)
    idx_ref[...] = idx_out
    gate_ref[...] = val_out / jnp.sum(val_out, axis=1, keepdims=True)


def _outproj_router(yc, ys, ya, h, w_out, g_att, norm_ffn, router_w, router_b, bm=512):
    L, D = h.shape
    full = lambda shape: pl.BlockSpec(shape, lambda i: (0,) * len(shape))
    return pl.pallas_call(
        _outproj_kernel,
        grid=(L // bm,),
        in_specs=[
            pl.BlockSpec((bm, C_CONV), lambda i: (i, 0)),
            pl.BlockSpec((bm, C_SSM), lambda i: (i, 0)),
            pl.BlockSpec((bm, D_ATT), lambda i: (i, 0)),
            pl.BlockSpec((bm, D), lambda i: (i, 0)),
            pl.BlockSpec((C_CONV, D), lambda i: (0, 0)),
            pl.BlockSpec((C_SSM, D), lambda i: (1, 0)),
            pl.BlockSpec((D_ATT, D), lambda i: (1, 0)),
            full((1, D_ATT)),
            full((1, D)),
            full((D, N_EXPERTS)),
            full((1, N_EXPERTS)),
        ],
        out_specs=[
            pl.BlockSpec((bm, D), lambda i: (i, 0)),
            pl.BlockSpec((bm, D), lambda i: (i, 0)),
            pl.BlockSpec((bm, TOP_K), lambda i: (i, 0)),
            pl.BlockSpec((bm, TOP_K), lambda i: (i, 0)),
        ],
        out_shape=[
            jax.ShapeDtypeStruct((L, D), F32),
            jax.ShapeDtypeStruct((L, D), BF16),
            jax.ShapeDtypeStruct((L, TOP_K), jnp.int32),
            jax.ShapeDtypeStruct((L, TOP_K), F32),
        ],
        compiler_params=_cparams(("parallel",)),
        name="outproj_router",
    )(yc, ys, ya, h, w_out, w_out, w_out, g_att.reshape(1, D_ATT),
      norm_ffn.reshape(1, D), router_w, router_b.reshape(1, N_EXPERTS))


GU_GROUP = 2 * LANES
MOE_CHUNKS = 2


def _gu_perm():
    src = np.arange(GU_GROUP)
    dst = np.where(src % 2 == 0, src // 2, LANES + src // 2)
    p = np.zeros((GU_GROUP, GU_GROUP), np.float32)
    p[src, dst] = 1.0
    return jnp.asarray(p, BF16)


def _wprep_kernel(w_ref, p_ref, o_ref):
    perm = p_ref[...]
    for c in range(w_ref.shape[1] // GU_GROUP):
        blk = w_ref[:, c * GU_GROUP:(c + 1) * GU_GROUP].astype(BF16)
        o_ref[0, :, c * GU_GROUP:(c + 1) * GU_GROUP] = jnp.dot(
            blk, perm, preferred_element_type=F32).astype(BF16)


def _prep_gate_up(w_gu_all, layer, br=1024, bc=1024):
    _, E, D, N = w_gu_all.shape
    return pl.pallas_call(
        _wprep_kernel,
        grid=(E, D // br, N // bc),
        in_specs=[
            pl.BlockSpec((None, None, br, bc), lambda e, i, j: (layer, e, i, j)),
            pl.BlockSpec((GU_GROUP, GU_GROUP), lambda e, i, j: (0, 0)),
        ],
        out_specs=pl.BlockSpec((1, br, bc), lambda e, i, j: (e, i, j)),
        out_shape=jax.ShapeDtypeStruct((E, D, N), BF16),
        compiler_params=_cparams(("parallel", "parallel", "parallel")),
        name="prep_gate_up",
    )(w_gu_all, _gu_perm())


def _moe_kernel(be_ref, nu_ref, x_ref, wgu_ref, bgu_ref, wdn_ref, bdn_ref, g_ref, *rest):
    o_ref, acc_ref = rest[-2:]
    b = pl.program_id(0)
    j = pl.program_id(1)
    nj = pl.num_programs(1)
    used = b < nu_ref[0]

    @pl.when(jnp.logical_and(jnp.logical_not(used), j == nj - 1))
    def _():
        o_ref[...] = jnp.zeros_like(o_ref)

    @pl.when(used)
    def _():
        @pl.when(j == 0)
        def _():
            acc_ref[...] = jnp.zeros_like(acc_ref)

        hgu = jnp.dot(x_ref[...], wgu_ref[0], preferred_element_type=F32) + bgu_ref[0]
        parts = []
        for c in range(hgu.shape[1] // GU_GROUP):
            g = jnp.minimum(hgu[:, c * GU_GROUP:c * GU_GROUP + LANES], SWIGLU_LIMIT)
            u = jnp.clip(hgu[:, c * GU_GROUP + LANES:(c + 1) * GU_GROUP],
                         -SWIGLU_LIMIT, SWIGLU_LIMIT)
            parts.append((g * jax.nn.sigmoid(SWIGLU_ALPHA * g) * (u + 1.0)).astype(BF16))
        a = jnp.concatenate(parts, axis=1)
        acc_ref[...] += jnp.dot(a, wdn_ref[...].astype(BF16), preferred_element_type=F32)

        @pl.when(j == nj - 1)
        def _():
            o_ref[...] = ((acc_ref[...] + bdn_ref[0]) * g_ref[...]).astype(o_ref.dtype)


def _moe_experts(xs, block_e, n_used, wgu_p, bgu_p, w_dn_all, layer, b_dn, row_gate, bm, tf=512,
                 first_block=0, total_rows=None, ys_prev=None):
    rows, D = xs.shape
    R = total_rows or rows
    _, E, d_ff, _ = w_dn_all.shape
    nb = rows // bm
    nj = d_ff // tf

    def blk(b, nu):
        return jnp.maximum(jnp.minimum(b, nu[0] - 1), 0)

    args = [block_e, n_used, xs, wgu_p, bgu_p.reshape(E, 1, 2 * d_ff), w_dn_all,
            b_dn.reshape(E, 1, D), row_gate]
    extra_specs, aliases = [], {}
    if ys_prev is not None:
        extra_specs = [pl.BlockSpec(memory_space=pl.ANY)]
        aliases = {len(args): 0}
        args.append(ys_prev)

    grid_spec = pltpu.PrefetchScalarGridSpec(
        num_scalar_prefetch=2,
        grid=(nb, nj),
        in_specs=[
            pl.BlockSpec((bm, D), lambda b, j, be, nu: (blk(b, nu), 0)),
            pl.BlockSpec((1, D, 2 * tf), lambda b, j, be, nu: (be[blk(b, nu)], 0, j)),
            pl.BlockSpec((1, 1, 2 * tf), lambda b, j, be, nu: (be[blk(b, nu)], 0, j)),
            pl.BlockSpec((None, None, tf, D),
                         lambda b, j, be, nu: (layer, be[blk(b, nu)], j, 0)),
            pl.BlockSpec((1, 1, D), lambda b, j, be, nu: (be[blk(b, nu)], 0, 0)),
            pl.BlockSpec((bm, 1), lambda b, j, be, nu: (blk(b, nu), 0)),
        ] + extra_specs,
        out_specs=pl.BlockSpec((bm, D), lambda b, j, be, nu: (b + first_block, 0)),
        scratch_shapes=[pltpu.VMEM((bm, D), F32)],
    )
    return pl.pallas_call(
        _moe_kernel,
        grid_spec=grid_spec,
        out_shape=jax.ShapeDtypeStruct((R, D), F32),
        input_output_aliases=aliases,
        compiler_params=_cparams(("arbitrary", "arbitrary")),
        name="moe_experts",
    )(*args)


def _moe(h, hn, top_idx, gates, w_gu_all, layer, b_gu, w_dn_all, b_dn, bm=512):
    T, D = h.shape
    E = N_EXPERTS
    n_rows = T * TOP_K
    flat_e = top_idx.reshape(-1)
    flat_g = gates.reshape(-1)
    order = jnp.argsort(flat_e, stable=True).astype(jnp.int32)
    rank = jnp.argsort(order).astype(jnp.int32)
    counts = jnp.sum((flat_e[None, :] == jnp.arange(E)[:, None]).astype(jnp.int32), axis=1)
    padded = (counts + bm - 1) // bm * bm
    start = jnp.cumsum(counts) - counts
    pend = jnp.cumsum(padded)
    pstart = pend - padded
    n_blocks = -(-n_rows // bm) + E
    R = n_blocks * bm
    block_e = jnp.minimum(
        jnp.sum((pend[None, :] <= (jnp.arange(n_blocks) * bm)[:, None]).astype(jnp.int32), axis=1),
        E - 1).astype(jnp.int32)
    n_used = (pend[-1:] // bm).astype(jnp.int32)
    row_e = jnp.repeat(block_e, bm)
    within = jnp.arange(R, dtype=jnp.int32) - pstart[row_e].astype(jnp.int32)
    valid = within < counts[row_e]
    src = order[jnp.clip(start[row_e].astype(jnp.int32) + within, 0, n_rows - 1)]
    buf_tok = jnp.where(valid, src // TOP_K, 0)
    buf_g = jnp.where(valid, flat_g[src], 0.0)
    pos = (rank + (pstart - start).astype(jnp.int32)[flat_e]).reshape(T, TOP_K)

    wgu_p = _prep_gate_up(w_gu_all, layer)
    bgu_p = b_gu.reshape(E, -1, LANES, 2).transpose(0, 1, 3, 2).reshape(E, -1)
    cb = n_blocks // MOE_CHUNKS
    ys = None
    for c in range(MOE_CHUNKS):
        rows = slice(c * cb * bm, (c + 1) * cb * bm)
        ys = _moe_experts(
            hn[buf_tok[rows]], block_e[c * cb:(c + 1) * cb], jnp.clip(n_used - c * cb, 0, cb),
            wgu_p, bgu_p, w_dn_all, layer, b_dn, buf_g[rows].reshape(-1, 1), bm,
            first_block=c * cb, total_rows=R, ys_prev=ys)
    return _moe_combine(h, ys[pos.T])


def _combine_kernel(h_ref, y_ref, o_ref):
    acc = h_ref[...]
    for k in range(TOP_K):
        acc = acc + y_ref[k]
    o_ref[...] = acc


def _moe_combine(h, y4, tb=256):
    T, D = h.shape
    return pl.pallas_call(
        _combine_kernel,
        grid=(T // tb,),
        in_specs=[
            pl.BlockSpec((tb, D), lambda i: (i, 0)),
            pl.BlockSpec((TOP_K, tb, D), lambda i: (0, i, 0)),
        ],
        out_specs=pl.BlockSpec((tb, D), lambda i: (i, 0)),
        out_shape=jax.ShapeDtypeStruct((T, D), F32),
        compiler_params=_cparams(("parallel",)),
        name="moe_combine",
    )(h, y4)


def _layer(h, p, layer, w_gu_all, w_dn_all):
    w_in = p['w_in']
    scale = LOG2E * HEAD_DIM ** -0.5
    z, f_t = _inproj(
        h, p['norm_mix'].reshape(1, -1),
        w_in[:, :Z_WIDTH].astype(BF16),
        w_in[:, Z_WIDTH:].T.astype(BF16),
        (p['q_norm'] * scale).reshape(1, HEAD_DIM), p['k_norm'].reshape(1, HEAD_DIM))
    c = _cumgate(f_t, p['b_forget'])
    g_conv = p['branch_norm'][:C_CONV]
    g_ssm = p['branch_norm'][C_CONV:C_CONV + C_SSM]
    g_att = p['branch_norm'][C_CONV + C_SSM:]
    yc = _conv_module(z, p['conv_w'], p['conv_b'], p['conv_ln_g'], p['conv_ln_b'], g_conv)
    ys = _s5_module(z, p['ssm_A_re'], p['ssm_A_im'], p['ssm_log_dt'], p['ssm_B_re'],
                    p['ssm_B_im'], p['ssm_C_re'], p['ssm_C_im'], p['ssm_D'],
                    p['ssm_glu_w'], p['ssm_glu_b'], g_ssm)
    ya = _fox_attention(z, c)
    h, hn, top_idx, gates = _outproj_router(
        yc, ys, ya, h, p['w_out'].astype(BF16), g_att, p['norm_ffn'],
        p['router_w'], p['router_b'])
    return _moe(h, hn, top_idx, gates, w_gu_all, layer, p['b_gate_up'], w_dn_all, p['b_down'])


def kernel(x, norm_mix, w_in, b_forget, conv_w, conv_b, conv_ln_g, conv_ln_b, ssm_A_re,
           ssm_A_im, ssm_log_dt, ssm_B_re, ssm_B_im, ssm_C_re, ssm_C_im, ssm_D, ssm_glu_w,
           ssm_glu_b, q_norm, k_norm, branch_norm, w_out, norm_ffn, router_w, router_b,
           w_gate_up, b_gate_up, w_down, b_down):
    params = dict(
        norm_mix=norm_mix, w_in=w_in, b_forget=b_forget, conv_w=conv_w, conv_b=conv_b,
        conv_ln_g=conv_ln_g, conv_ln_b=conv_ln_b, ssm_A_re=ssm_A_re, ssm_A_im=ssm_A_im,
        ssm_log_dt=ssm_log_dt, ssm_B_re=ssm_B_re, ssm_B_im=ssm_B_im, ssm_C_re=ssm_C_re,
        ssm_C_im=ssm_C_im, ssm_D=ssm_D, ssm_glu_w=ssm_glu_w, ssm_glu_b=ssm_glu_b,
        q_norm=q_norm, k_norm=k_norm, branch_norm=branch_norm, w_out=w_out,
        norm_ffn=norm_ffn, router_w=router_w, router_b=router_b,
        b_gate_up=b_gate_up, b_down=b_down)
    bsz, L, D = x.shape
    assert bsz == 1
    h = x.reshape(L, D)
    for l in range(norm_mix.shape[0]):
        h = _layer(h, {k: v[l] for k, v in params.items()}, l, w_gate_up, w_down)
    return h.reshape(bsz, L, D)
```
